```python
import jax, jax.numpy as jnp
from jax import lax
import numpy as np

D_MODEL = 1024
BATCH = 32
SEQ = 2048
DEPTH = 1

CHUNK = 64
LRU_WIDTH = 1280
LRU_HEADS = 10
LRU_HEAD_DIM = LRU_WIDTH // LRU_HEADS
CONV_WIDTH = 4
LRU_C = 8.0
SGU_WIDTH = 768
SGU_GROUPS = 6
SGU_GROUP_DIM = SGU_WIDTH // SGU_GROUPS
SGU_BLOCK = 128
D_FF = 4 * D_MODEL
N_BRANCH = 2
D_IN = 2 * LRU_WIDTH + 2 * SGU_WIDTH + N_BRANCH * D_MODEL
IN_SPLITS = (LRU_WIDTH, 2 * LRU_WIDTH, 2 * LRU_WIDTH + SGU_WIDTH,
             2 * LRU_WIDTH + 2 * SGU_WIDTH, 2 * LRU_WIDTH + 2 * SGU_WIDTH + D_MODEL)
ALPHA = (2.0 * DEPTH) ** 0.25
BETA = (8.0 * DEPTH) ** -0.25
LN_EPS = 1e-5

kernel_name = "hawk_gmlp_hybrid_deepnorm_adaln"


def _layer_norm(x, g, b):
    xf = x.astype(jnp.float32)
    mu = jnp.mean(xf, axis=-1, keepdims=True)
    var = jnp.mean(jnp.square(xf - mu), axis=-1, keepdims=True)
    y = (xf - mu) * lax.rsqrt(var + LN_EPS)
    return (y * g.astype(jnp.float32) + b.astype(jnp.float32)).astype(x.dtype)


def _causal_depthwise_conv(x, w, b):
    y = lax.conv_general_dilated(
        x, w[:, None, :].astype(x.dtype), window_strides=(1,),
        padding=[(CONV_WIDTH - 1, 0)], dimension_numbers=("NWC", "WIO", "NWC"),
        feature_group_count=x.shape[-1])
    return y + b


def _rg_lru(x, w_a, b_a, w_x, b_x, lam):
    B, S, _ = x.shape
    xh = x.reshape(B, S, LRU_HEADS, LRU_HEAD_DIM)
    r = jax.nn.sigmoid(jnp.einsum("bshi,hij->bshj", xh, w_a).reshape(B, S, LRU_WIDTH) + b_a)
    i = jax.nn.sigmoid(jnp.einsum("bshi,hij->bshj", xh, w_x).reshape(B, S, LRU_WIDTH) + b_x)
    log_a = (-LRU_C * jax.nn.softplus(-lam.astype(jnp.float32))) * r.astype(jnp.float32)
    a = jnp.exp(log_a)
    inp = jnp.sqrt(-jnp.expm1(2.0 * log_a)) * (i * x).astype(jnp.float32)

    def step(h, ab):
        a_t, b_t = ab
        h = a_t * h + b_t
        return h, h

    h0 = jnp.zeros((B, LRU_WIDTH), jnp.float32)
    _, hs = lax.scan(step, h0, (jnp.swapaxes(a, 0, 1), jnp.swapaxes(inp, 0, 1)))
    return jnp.swapaxes(hs, 0, 1).astype(x.dtype)


def _spatial_gating(u, v, w_sp, b_sp, ln_g, ln_b):
    B, S, _ = u.shape
    v = _layer_norm(v, ln_g, ln_b)
    nblk = S // SGU_BLOCK
    vb = v.reshape(B, nblk, SGU_BLOCK, SGU_GROUPS, SGU_GROUP_DIM)
    pos = jnp.arange(SGU_BLOCK)
    mask = (pos[None, :] // CHUNK) <= (pos[:, None] // CHUNK)
    w = jnp.where(mask[None], w_sp, 0.0).astype(v.dtype)
    mixed = jnp.einsum("gts,bnsgc->bntgc", w, vb) + jnp.transpose(b_sp)[None, None, :, :, None]
    return u * mixed.reshape(B, S, SGU_WIDTH)


def setup_inputs(seed: int = 0) -> dict:
    key = jax.random.key(seed)
    ks = jax.random.split(key, 28)
    L = DEPTH
    nrm = lambda k, shape, s: jax.random.normal(k, shape, jnp.float32) * s
    u = jax.random.uniform(ks[12], (L, LRU_WIDTH), jnp.float32, 0.9, 0.999)
    a0 = u ** (1.0 / LRU_C)
    lru_lambda = jnp.log(a0) - jnp.log1p(-a0)
    return {
        "x": nrm(ks[0], (BATCH, SEQ, D_MODEL), 1.0),
        "c": nrm(ks[1], (BATCH, D_MODEL), 1.0),
        "w_ada": nrm(ks[2], (L, D_MODEL, 6 * D_MODEL), 0.1 * D_MODEL ** -0.5),
        "b_ada": nrm(ks[3], (L, 6 * D_MODEL), 0.01),
        "w_in": nrm(ks[4], (L, D_MODEL, D_IN), D_MODEL ** -0.5),
        "b_in": nrm(ks[5], (L, D_IN), 0.01),
        "w_conv": nrm(ks[6], (L, CONV_WIDTH, LRU_WIDTH), CONV_WIDTH ** -0.5),
        "b_conv": nrm(ks[7], (L, LRU_WIDTH), 0.01),
        "w_rg_a": nrm(ks[8], (L, LRU_HEADS, LRU_HEAD_DIM, LRU_HEAD_DIM), LRU_HEAD_DIM ** -0.5),
        "b_rg_a": nrm(ks[9], (L, LRU_WIDTH), 0.01),
        "w_rg_x": nrm(ks[10], (L, LRU_HEADS, LRU_HEAD_DIM, LRU_HEAD_DIM), LRU_HEAD_DIM ** -0.5),
        "b_rg_x": nrm(ks[11], (L, LRU_WIDTH), 0.01),
        "lru_lambda": lru_lambda,
        "w_sp": nrm(ks[13], (L, SGU_GROUPS, SGU_BLOCK, SGU_BLOCK), SGU_BLOCK ** -0.5),
        "b_sp": 1.0 + nrm(ks[14], (L, SGU_GROUPS, SGU_BLOCK), 0.01),
        "ln_v_g": 1.0 + nrm(ks[15], (L, SGU_WIDTH), 0.01),
        "ln_v_b": nrm(ks[16], (L, SGU_WIDTH), 0.01),
        "w_o_lru": nrm(ks[17], (L, LRU_WIDTH, D_MODEL), BETA * LRU_WIDTH ** -0.5),
        "w_o_sgu": nrm(ks[18], (L, SGU_WIDTH, D_MODEL), BETA * SGU_WIDTH ** -0.5),
        "w_out": nrm(ks[19], (L, D_MODEL, D_MODEL), BETA * D_MODEL ** -0.5),
        "ln1_g": 1.0 + nrm(ks[20], (L, D_MODEL), 0.01),
        "ln1_b": nrm(ks[21], (L, D_MODEL), 0.01),
        "w_up": nrm(ks[22], (L, D_MODEL, D_FF), BETA * D_MODEL ** -0.5),
        "w_down": nrm(ks[23], (L, D_FF, D_MODEL), BETA * D_FF ** -0.5),
        "ln2_g": 1.0 + nrm(ks[24], (L, D_MODEL), 0.01),
        "ln2_b": nrm(ks[25], (L, D_MODEL), 0.01),
    }


def reference(x, c, w_ada, b_ada, w_in, b_in, w_conv, b_conv, w_rg_a, b_rg_a, w_rg_x, b_rg_x,
              lru_lambda, w_sp, b_sp, ln_v_g, ln_v_b, w_o_lru, w_o_sgu, w_out, ln1_g, ln1_b,
              w_up, w_down, ln2_g, ln2_b):
    c_act = jax.nn.silu(c)
    for l in range(DEPTH):
        mod = c_act @ w_ada[l] + b_ada[l]
        sh1, sc1, gt1, sh2, sc2, gt2 = jnp.split(mod, 6, axis=-1)

        h = x * (1.0 + sc1[:, None, :]) + sh1[:, None, :]
        proj = h @ w_in[l] + b_in[l]
        x_lru, g_lru, u, v, gate_a, gate_b = jnp.split(proj, IN_SPLITS, axis=-1)

        xc = _causal_depthwise_conv(x_lru, w_conv[l], b_conv[l])
        y_lru = _rg_lru(xc, w_rg_a[l], b_rg_a[l], w_rg_x[l], b_rg_x[l], lru_lambda[l])
        y_a = (y_lru * jax.nn.gelu(g_lru)) @ w_o_lru[l]

        y_sgu = _spatial_gating(jax.nn.gelu(u), jax.nn.gelu(v), w_sp[l], b_sp[l], ln_v_g[l], ln_v_b[l])
        y_b = y_sgu @ w_o_sgu[l]

        merged = jax.nn.sigmoid(gate_a) * y_a + jax.nn.sigmoid(gate_b) * y_b
        mix = merged @ w_out[l]
        x = _layer_norm(ALPHA * x + (1.0 + gt1[:, None, :]) * mix, ln1_g[l], ln1_b[l])

        h2 = x * (1.0 + sc2[:, None, :]) + sh2[:, None, :]
        f = jnp.square(jax.nn.relu(h2 @ w_up[l])) @ w_down[l]
        x = _layer_norm(ALPHA * x + (1.0 + gt2[:, None, :]) * f, ln2_g[l], ln2_b[l])
    return x
```

```python
import functools

import jax
import jax.numpy as jnp
import numpy as np
from jax import lax
from jax.experimental import pallas as pl
from jax.experimental.pallas import tpu as pltpu

D_MODEL = 1024
LRU_WIDTH = 1280
LRU_HEADS = 10
LRU_HEAD_DIM = LRU_WIDTH // LRU_HEADS
CONV_WIDTH = 4
LRU_C = 8.0
SGU_WIDTH = 768
SGU_GROUPS = 6
SGU_GROUP_DIM = SGU_WIDTH // SGU_GROUPS
SGU_BLOCK = 128
CHUNK = 64
D_FF = 4 * D_MODEL
D_IN = 2 * LRU_WIDTH + 2 * SGU_WIDTH + 2 * D_MODEL
ALPHA = 2.0 ** 0.25
LN_EPS = 1e-5

_O_XLRU, _O_GLRU, _O_U, _O_V, _O_GA, _O_GB = 0, 1280, 2560, 3328, 4096, 5120

V7X_F32_SUBLANES = 8
BG = V7X_F32_SUBLANES
TT = CHUNK
ROWS = TT * BG
HALO = (CONV_WIDTH - 1) * BG
MLP_ROWS = 1024
FF_CHUNK = 1024
V7X_VMEM_LIMIT = 60 * 1024 * 1024

_F32 = jnp.float32
_BF16 = jnp.bfloat16
_GELU_C = np.sqrt(2.0 / np.pi).astype(np.float32)


def _sigmoid(x):
    return 0.5 * jnp.tanh(0.5 * x) + 0.5


def _gelu(x):
    return x * (0.5 * (1.0 + jnp.tanh(_GELU_C * (x + 0.044715 * (x * x * x)))))


def _layer_norm(x, g, b):
    mu = jnp.mean(x, axis=-1, keepdims=True)
    xc = x - mu
    var = jnp.mean(xc * xc, axis=-1, keepdims=True)
    return xc * lax.rsqrt(var + LN_EPS) * g + b


def _per_seq(x, m, op):
    rows, d = x.shape
    return op(x.reshape(rows // BG, BG, d), m[None]).reshape(rows, d)


def _modulate(x, shift, scale):
    return _per_seq(_per_seq(x, 1.0 + scale, jnp.multiply), shift, jnp.add)


def _dot(a, b):
    return jnp.dot(a, b, preferred_element_type=_F32)


def _ada_kernel(c_ref, w_ref, b_ref, o_ref):
    c = c_ref[...]
    c_act = (c * _sigmoid(c)).astype(_BF16)
    o_ref[...] = _dot(c_act, w_ref[...].astype(_BF16)) + b_ref[...]


def _ada(c, w_ada, b_ada):
    b, d = c.shape
    n = w_ada.shape[1]
    bn = D_MODEL
    return pl.pallas_call(
        _ada_kernel,
        grid=(n // bn,),
        in_specs=[
            pl.BlockSpec((b, d), lambda j: (0, 0)),
            pl.BlockSpec((d, bn), lambda j: (0, j)),
            pl.BlockSpec((1, bn), lambda j: (0, j)),
        ],
        out_specs=pl.BlockSpec((b, bn), lambda j: (0, j)),
        out_shape=jax.ShapeDtypeStruct((b, n), _F32),
        name="ada_mod",
    )(c, w_ada, b_ada.reshape(1, n))


def _mixer_kernel(x_ref, mod_ref, w_in_ref, b_in_ref, w_conv_ref, b_conv_ref, w_rg_ref, b_rg_ref,
                  lam_ref, w_sp_ref, b_sp_ref, lnv_g_ref, lnv_b_ref, w_ol_ref, w_os_ref, w_out_ref,
                  ln1_g_ref, ln1_b_ref, o_ref, xl_s, a_s, h_s, hc_s, v_s, m_s):
    i = pl.program_id(1)

    @pl.when(i == 0)
    def _():
        xl_s[0:HALO, :] = jnp.zeros((HALO, LRU_WIDTH), _F32)
        hc_s[...] = jnp.zeros((BG, LRU_WIDTH), _F32)

    x = x_ref[...]
    shift = mod_ref[:, 0:D_MODEL]
    scale = mod_ref[:, D_MODEL:2 * D_MODEL]
    gate = mod_ref[:, 2 * D_MODEL:3 * D_MODEL]
    hm = _modulate(x, shift, scale).astype(_BF16)

    def proj(lo, hi):
        return _dot(hm, w_in_ref[:, lo:hi]) + b_in_ref[:, lo:hi]

    xl_s[HALO:HALO + ROWS, :] = proj(_O_XLRU, _O_GLRU)
    xc = b_conv_ref[...] + w_conv_ref[0:1, :] * xl_s[0:ROWS, :]
    for k in range(1, CONV_WIDTH):
        xc = xc + w_conv_ref[k:k + 1, :] * xl_s[k * BG:k * BG + ROWS, :]
    xl_s[0:HALO, :] = xl_s[ROWS:ROWS + HALO, :]

    z = -lam_ref[...]
    c_lam = -LRU_C * (jnp.maximum(z, 0.0) + jnp.log1p(jnp.exp(-jnp.abs(z))))
    xcb = xc.astype(_BF16)
    for h in range(LRU_HEADS):
        sl = slice(h * LRU_HEAD_DIM, (h + 1) * LRU_HEAD_DIM)
        ri = _dot(xcb[:, sl], w_rg_ref[h])
        r = _sigmoid(ri[:, 0:LRU_HEAD_DIM] + b_rg_ref[0:1, sl])
        ig = _sigmoid(ri[:, LRU_HEAD_DIM:] + b_rg_ref[1:2, sl])
        a = jnp.exp(c_lam[:, sl] * r)
        m2 = 1.0 - a * a
        mult = jnp.where(m2 > 0.0, m2 * lax.rsqrt(m2), 0.0)
        a_s[:, sl] = a
        h_s[:, sl] = mult * (ig * xc[:, sl])

    hprev = hc_s[...]
    for t in range(TT):
        rows = slice(t * BG, (t + 1) * BG)
        hprev = a_s[rows, :] * hprev + h_s[rows, :]
        h_s[rows, :] = hprev
    hc_s[...] = hprev

    g_lru = _gelu(proj(_O_GLRU, _O_U))
    y_a = _dot((h_s[...] * g_lru).astype(_BF16), w_ol_ref[...])

    u = _gelu(proj(_O_U, _O_V))
    v = _layer_norm(_gelu(proj(_O_V, _O_GA)), lnv_g_ref[...], lnv_b_ref[...])

    def seq_rows(b):
        return pl.ds(b, TT, stride=BG)

    def spatial_mix(parity):
        for g in range(SGU_GROUPS):
            v_s[parity, g] = v[:, g * SGU_GROUP_DIM:(g + 1) * SGU_GROUP_DIM]
        for g in range(SGU_GROUPS):
            if parity == 0:
                w = w_sp_ref[g, 0:CHUNK, 0:CHUNK]
                vs = [v_s[0, g, seq_rows(b), :] for b in range(BG)]
            else:
                w = w_sp_ref[g, CHUNK:SGU_BLOCK, :]
                vs = [jnp.concatenate([v_s[0, g, seq_rows(b), :], v_s[1, g, seq_rows(b), :]], axis=0)
                      for b in range(BG)]
            vmat = jnp.concatenate(vs, axis=1).astype(_BF16)
            out = _dot(w.astype(_BF16), vmat)
            bias = b_sp_ref[g, parity * CHUNK:(parity + 1) * CHUNK, :]
            for b in range(BG):
                m_s[g, seq_rows(b), :] = out[:, b * SGU_GROUP_DIM:(b + 1) * SGU_GROUP_DIM] + bias

    parity = i % 2
    pl.when(parity == 0)(functools.partial(spatial_mix, 0))
    pl.when(parity == 1)(functools.partial(spatial_mix, 1))

    mixed = jnp.concatenate([m_s[g] for g in range(SGU_GROUPS)], axis=1)
    y_b = _dot((u * mixed).astype(_BF16), w_os_ref[...])

    merged = _sigmoid(proj(_O_GA, _O_GB)) * y_a + _sigmoid(proj(_O_GB, D_IN)) * y_b
    mix = _dot(merged.astype(_BF16), w_out_ref[...])
    res = ALPHA * x + _per_seq(mix, 1.0 + gate, jnp.multiply)
    o_ref[...] = _layer_norm(res, ln1_g_ref[...], ln1_b_ref[...])


def _const_spec(shape):
    nd = len(shape)
    return pl.BlockSpec(shape, lambda *_: (0,) * nd, pipeline_mode=pl.Buffered(1))


def _mixer(xt, mod, w_in, b_in, w_conv, b_conv, w_rg, b_rg, lam, w_sp, b_sp, lnv_g, lnv_b,
           w_ol, w_os, w_out, ln1_g, ln1_b):
    ngroups, nrows, d = xt.shape
    consts = (w_in, b_in, w_conv, b_conv, w_rg, b_rg, lam, w_sp, b_sp, lnv_g, lnv_b,
              w_ol, w_os, w_out, ln1_g, ln1_b)
    return pl.pallas_call(
        _mixer_kernel,
        grid=(ngroups, nrows // ROWS),
        in_specs=[
            pl.BlockSpec((None, ROWS, d), lambda g, i: (g, i, 0)),
            pl.BlockSpec((None, BG, 3 * d), lambda g, i: (g, 0, 0)),
        ] + [_const_spec(a.shape) for a in consts],
        out_specs=pl.BlockSpec((None, ROWS, d), lambda g, i: (g, i, 0)),
        out_shape=jax.ShapeDtypeStruct(xt.shape, _F32),
        scratch_shapes=[
            pltpu.VMEM((HALO + ROWS, LRU_WIDTH), _F32),
            pltpu.VMEM((ROWS, LRU_WIDTH), _F32),
            pltpu.VMEM((ROWS, LRU_WIDTH), _F32),
            pltpu.VMEM((BG, LRU_WIDTH), _F32),
            pltpu.VMEM((2, SGU_GROUPS, ROWS, SGU_GROUP_DIM), _F32),
            pltpu.VMEM((SGU_GROUPS, ROWS, SGU_GROUP_DIM), _F32),
        ],
        compiler_params=pltpu.CompilerParams(
            dimension_semantics=("arbitrary", "arbitrary"),
            vmem_limit_bytes=V7X_VMEM_LIMIT,
        ),
        name="token_mixer",
    )(xt, mod, *consts)


def _mlp_kernel(x_ref, mod_ref, w_up_ref, w_down_ref, g_ref, b_ref, o_ref):
    x = x_ref[...]
    shift = mod_ref[:, 0:D_MODEL]
    scale = mod_ref[:, D_MODEL:2 * D_MODEL]
    gate = mod_ref[:, 2 * D_MODEL:3 * D_MODEL]
    h2 = _modulate(x, shift, scale).astype(_BF16)
    acc = None
    for k in range(D_FF // FF_CHUNK):
        cols = slice(k * FF_CHUNK, (k + 1) * FF_CHUNK)
        f = jnp.square(jnp.maximum(_dot(h2, w_up_ref[:, cols]), 0.0)).astype(_BF16)
        part = _dot(f, w_down_ref[cols, :])
        acc = part if acc is None else acc + part
    res = ALPHA * x + _per_seq(acc, 1.0 + gate, jnp.multiply)
    o_ref[...] = _layer_norm(res, g_ref[...], b_ref[...])


def _mlp(xt, mod, w_up, w_down, ln_g, ln_b):
    ngroups, nrows, d = xt.shape
    return pl.pallas_call(
        _mlp_kernel,
        grid=(ngroups, nrows // MLP_ROWS),
        in_specs=[
            pl.BlockSpec((None, MLP_ROWS, d), lambda g, i: (g, i, 0)),
            pl.BlockSpec((None, BG, 3 * d), lambda g, i: (g, 0, 1)),
            _const_spec(w_up.shape),
            _const_spec(w_down.shape),
            _const_spec(ln_g.shape),
            _const_spec(ln_b.shape),
        ],
        out_specs=pl.BlockSpec((None, MLP_ROWS, d), lambda g, i: (g, i, 0)),
        out_shape=jax.ShapeDtypeStruct(xt.shape, _F32),
        compiler_params=pltpu.CompilerParams(
            dimension_semantics=("arbitrary", "arbitrary"),
            vmem_limit_bytes=V7X_VMEM_LIMIT,
        ),
        name="channel_mlp",
    )(xt, mod, w_up, w_down, ln_g, ln_b)


def kernel(x, c, w_ada, b_ada, w_in, b_in, w_conv, b_conv, w_rg_a, b_rg_a, w_rg_x, b_rg_x, lru_lambda,
           w_sp, b_sp, ln_v_g, ln_v_b, w_o_lru, w_o_sgu, w_out, ln1_g, ln1_b, w_up, w_down, ln2_g, ln2_b):
    batch, seq, d = x.shape
    depth = w_in.shape[0]
    ngroups = batch // BG
    row = lambda a: a.reshape(1, -1)

    xt = x.reshape(ngroups, BG, seq, d).transpose(0, 2, 1, 3).reshape(ngroups, seq * BG, d)
    for l in range(depth):
        mod = _ada(c, w_ada[l], b_ada[l]).reshape(ngroups, BG, 6 * d)
        w_rg = jnp.concatenate([w_rg_a[l], w_rg_x[l]], axis=-1).astype(_BF16)
        b_rg = jnp.stack([b_rg_a[l], b_rg_x[l]])
        b_sp_lanes = jnp.broadcast_to(b_sp[l][:, :, None], (SGU_GROUPS, SGU_BLOCK, SGU_GROUP_DIM))
        xt = _mixer(xt, mod, w_in[l].astype(_BF16), row(b_in[l]), w_conv[l], row(b_conv[l]), w_rg, b_rg,
                    row(lru_lambda[l]), w_sp[l], b_sp_lanes, row(ln_v_g[l]), row(ln_v_b[l]),
                    w_o_lru[l].astype(_BF16), w_o_sgu[l].astype(_BF16), w_out[l].astype(_BF16),
                    row(ln1_g[l]), row(ln1_b[l]))
        xt = _mlp(xt, mod, w_up[l].astype(_BF16), w_down[l].astype(_BF16), row(ln2_g[l]), row(ln2_b[l]))
    return xt.reshape(ngroups, seq, BG, d).transpose(0, 2, 1, 3).reshape(batch, seq, d)
```

```python
import functools

import jax
import jax.numpy as jnp
import numpy as np
from jax import lax
from jax.experimental import pallas as pl
from jax.experimental.pallas import tpu as pltpu

D_MODEL = 1024
LRU_WIDTH = 1280
LRU_HEADS = 10
LRU_HEAD_DIM = LRU_WIDTH // LRU_HEADS
CONV_WIDTH = 4
LRU_C = 8.0
SGU_WIDTH = 768
SGU_GROUPS = 6
SGU_GROUP_DIM = SGU_WIDTH // SGU_GROUPS
SGU_BLOCK = 128
CHUNK = 64
D_FF = 4 * D_MODEL
D_IN = 2 * LRU_WIDTH + 2 * SGU_WIDTH + 2 * D_MODEL
ALPHA = 2.0 ** 0.25
LN_EPS = 1e-5

_O_XLRU, _O_GLRU, _O_U, _O_V, _O_GA, _O_GB = 0, 1280, 2560, 3328, 4096, 5120

V7X_F32_SUBLANES = 8
BG = V7X_F32_SUBLANES
TT = CHUNK
ROWS = TT * BG
HALO = (CONV_WIDTH - 1) * BG
MLP_ROWS = 1024
MLP_TT = MLP_ROWS // BG
FF_CHUNK = 1024
V7X_VMEM_LIMIT = 60 * 1024 * 1024

_F32 = jnp.float32
_BF16 = jnp.bfloat16
_GELU_C = np.sqrt(2.0 / np.pi).astype(np.float32)


def _sigmoid(x):
    return 0.5 * jnp.tanh(0.5 * x) + 0.5


def _gelu(x):
    return x * (0.5 * (1.0 + jnp.tanh(_GELU_C * (x + 0.044715 * (x * x * x)))))


def _layer_norm(x, g, b):
    mu = jnp.mean(x, axis=-1, keepdims=True)
    xc = x - mu
    var = jnp.mean(xc * xc, axis=-1, keepdims=True)
    return xc * lax.rsqrt(var + LN_EPS) * g + b


def _per_seq(x, m, op):
    rows, d = x.shape
    return op(x.reshape(rows // BG, BG, d), m[None]).reshape(rows, d)


def _modulate(x, shift, scale):
    return _per_seq(_per_seq(x, 1.0 + scale, jnp.multiply), shift, jnp.add)


def _dot(a, b):
    return jnp.dot(a, b, preferred_element_type=_F32)


def _ada_kernel(c_ref, w_ref, b_ref, o_ref):
    c = c_ref[...]
    c_act = (c * _sigmoid(c)).astype(_BF16)
    o_ref[...] = _dot(c_act, w_ref[...].astype(_BF16)) + b_ref[...]


def _ada(c, w_ada, b_ada):
    b, d = c.shape
    n = w_ada.shape[1]
    bn = D_MODEL
    return pl.pallas_call(
        _ada_kernel,
        grid=(n // bn,),
        in_specs=[
            pl.BlockSpec((b, d), lambda j: (0, 0)),
            pl.BlockSpec((d, bn), lambda j: (0, j)),
            pl.BlockSpec((1, bn), lambda j: (0, j)),
        ],
        out_specs=pl.BlockSpec((b, bn), lambda j: (0, j)),
        out_shape=jax.ShapeDtypeStruct((b, n), _F32),
        name="ada_mod",
    )(c, w_ada, b_ada.reshape(1, n))


def _seq_tile_copies(hbm_ref, buf_ref, sem_ref, step, tiles_per_group, tile_steps, to_hbm):
    g = step // tiles_per_group
    t0 = (step % tiles_per_group) * tile_steps
    slot = step % 2
    copies = []
    for b in range(BG):
        hbm = hbm_ref.at[g * BG + b, pl.ds(t0, tile_steps), :]
        buf = buf_ref.at[slot, :, b, :]
        src, dst = (buf, hbm) if to_hbm else (hbm, buf)
        copies.append(pltpu.make_async_copy(src, dst, sem_ref.at[slot]))
    return copies


def _mixer_kernel(x_hbm, mod_ref, w_in_ref, b_in_ref, w_conv_ref, b_conv_ref, w_rg_ref, b_rg_ref,
                  lam_ref, w_sp_ref, b_sp_ref, lnv_g_ref, lnv_b_ref, w_ol_ref, w_os_ref, w_out_ref,
                  ln1_g_ref, ln1_b_ref, o_ref, x_buf, x_sem, xl_s, a_s, h_s, hc_s, v_s, m_s):
    i = pl.program_id(1)
    tiles_per_group = pl.num_programs(1)
    step = pl.program_id(0) * tiles_per_group + i
    last_step = pl.num_programs(0) * tiles_per_group - 1
    x_copies = functools.partial(_seq_tile_copies, x_hbm, x_buf, x_sem,
                                 tiles_per_group=tiles_per_group, tile_steps=TT, to_hbm=False)

    @pl.when(step == 0)
    def _():
        for cp in x_copies(step):
            cp.start()

    @pl.when(step < last_step)
    def _():
        for cp in x_copies(step + 1):
            cp.start()

    for cp in x_copies(step):
        cp.wait()

    @pl.when(i == 0)
    def _():
        xl_s[0:HALO, :] = jnp.zeros((HALO, LRU_WIDTH), _F32)
        hc_s[...] = jnp.zeros((BG, LRU_WIDTH), _F32)

    x = x_buf[step % 2].reshape(ROWS, D_MODEL)
    shift = mod_ref[:, 0:D_MODEL]
    scale = mod_ref[:, D_MODEL:2 * D_MODEL]
    gate = mod_ref[:, 2 * D_MODEL:3 * D_MODEL]
    hm = _modulate(x, shift, scale).astype(_BF16)

    def proj(lo, hi):
        return _dot(hm, w_in_ref[:, lo:hi]) + b_in_ref[:, lo:hi]

    xl_s[HALO:HALO + ROWS, :] = proj(_O_XLRU, _O_GLRU)
    xc = b_conv_ref[...] + w_conv_ref[0:1, :] * xl_s[0:ROWS, :]
    for k in range(1, CONV_WIDTH):
        xc = xc + w_conv_ref[k:k + 1, :] * xl_s[k * BG:k * BG + ROWS, :]
    xl_s[0:HALO, :] = xl_s[ROWS:ROWS + HALO, :]

    z = -lam_ref[...]
    c_lam = -LRU_C * (jnp.maximum(z, 0.0) + jnp.log1p(jnp.exp(-jnp.abs(z))))
    xcb = xc.astype(_BF16)
    for h in range(LRU_HEADS):
        sl = slice(h * LRU_HEAD_DIM, (h + 1) * LRU_HEAD_DIM)
        ri = _dot(xcb[:, sl], w_rg_ref[h])
        r = _sigmoid(ri[:, 0:LRU_HEAD_DIM] + b_rg_ref[0:1, sl])
        ig = _sigmoid(ri[:, LRU_HEAD_DIM:] + b_rg_ref[1:2, sl])
        a = jnp.exp(c_lam[:, sl] * r)
        m2 = 1.0 - a * a
        mult = jnp.where(m2 > 0.0, m2 * lax.rsqrt(m2), 0.0)
        a_s[:, sl] = a
        h_s[:, sl] = mult * (ig * xc[:, sl])

    hprev = hc_s[...]
    for t in range(TT):
        rows = slice(t * BG, (t + 1) * BG)
        hprev = a_s[rows, :] * hprev + h_s[rows, :]
        h_s[rows, :] = hprev
    hc_s[...] = hprev

    g_lru = _gelu(proj(_O_GLRU, _O_U))
    y_a = _dot((h_s[...] * g_lru).astype(_BF16), w_ol_ref[...])

    u = _gelu(proj(_O_U, _O_V))
    v = _layer_norm(_gelu(proj(_O_V, _O_GA)), lnv_g_ref[...], lnv_b_ref[...])

    def seq_rows(b):
        return pl.ds(b, TT, stride=BG)

    def spatial_mix(parity):
        for g in range(SGU_GROUPS):
            v_s[parity, g] = v[:, g * SGU_GROUP_DIM:(g + 1) * SGU_GROUP_DIM]
        for g in range(SGU_GROUPS):
            if parity == 0:
                w = w_sp_ref[g, 0:CHUNK, 0:CHUNK]
                vs = [v_s[0, g, seq_rows(b), :] for b in range(BG)]
            else:
                w = w_sp_ref[g, CHUNK:SGU_BLOCK, :]
                vs = [jnp.concatenate([v_s[0, g, seq_rows(b), :], v_s[1, g, seq_rows(b), :]], axis=0)
                      for b in range(BG)]
            vmat = jnp.concatenate(vs, axis=1).astype(_BF16)
            out = _dot(w.astype(_BF16), vmat)
            bias = b_sp_ref[g, parity * CHUNK:(parity + 1) * CHUNK, :]
            for b in range(BG):
                m_s[g, seq_rows(b), :] = out[:, b * SGU_GROUP_DIM:(b + 1) * SGU_GROUP_DIM] + bias

    parity = i % 2
    pl.when(parity == 0)(functools.partial(spatial_mix, 0))
    pl.when(parity == 1)(functools.partial(spatial_mix, 1))

    mixed = jnp.concatenate([m_s[g] for g in range(SGU_GROUPS)], axis=1)
    y_b = _dot((u * mixed).astype(_BF16), w_os_ref[...])

    merged = _sigmoid(proj(_O_GA, _O_GB)) * y_a + _sigmoid(proj(_O_GB, D_IN)) * y_b
    mix = _dot(merged.astype(_BF16), w_out_ref[...])
    res = ALPHA * x + _per_seq(mix, 1.0 + gate, jnp.multiply)
    o_ref[...] = _layer_norm(res, ln1_g_ref[...], ln1_b_ref[...])


def _const_spec(shape):
    nd = len(shape)
    return pl.BlockSpec(shape, lambda *_: (0,) * nd, pipeline_mode=pl.Buffered(1))


def _mixer(x, mod, w_in, b_in, w_conv, b_conv, w_rg, b_rg, lam, w_sp, b_sp, lnv_g, lnv_b,
           w_ol, w_os, w_out, ln1_g, ln1_b):
    batch, seq, d = x.shape
    ngroups = batch // BG
    consts = (w_in, b_in, w_conv, b_conv, w_rg, b_rg, lam, w_sp, b_sp, lnv_g, lnv_b,
              w_ol, w_os, w_out, ln1_g, ln1_b)
    return pl.pallas_call(
        _mixer_kernel,
        grid=(ngroups, seq // TT),
        in_specs=[
            pl.BlockSpec(memory_space=pl.ANY),
            pl.BlockSpec((None, BG, 3 * d), lambda g, i: (g, 0, 0)),
        ] + [_const_spec(a.shape) for a in consts],
        out_specs=pl.BlockSpec((None, ROWS, d), lambda g, i: (g, i, 0)),
        out_shape=jax.ShapeDtypeStruct((ngroups, seq * BG, d), _F32),
        scratch_shapes=[
            pltpu.VMEM((2, TT, BG, d), _F32),
            pltpu.SemaphoreType.DMA((2,)),
            pltpu.VMEM((HALO + ROWS, LRU_WIDTH), _F32),
            pltpu.VMEM((ROWS, LRU_WIDTH), _F32),
            pltpu.VMEM((ROWS, LRU_WIDTH), _F32),
            pltpu.VMEM((BG, LRU_WIDTH), _F32),
            pltpu.VMEM((2, SGU_GROUPS, ROWS, SGU_GROUP_DIM), _F32),
            pltpu.VMEM((SGU_GROUPS, ROWS, SGU_GROUP_DIM), _F32),
        ],
        compiler_params=pltpu.CompilerParams(
            dimension_semantics=("arbitrary", "arbitrary"),
            vmem_limit_bytes=V7X_VMEM_LIMIT,
        ),
        name="token_mixer",
    )(x, mod, *consts)


def _mlp_kernel(x_ref, mod_ref, w_up_ref, w_down_ref, g_ref, b_ref, o_hbm, o_buf, o_sem):
    tiles_per_group = pl.num_programs(1)
    step = pl.program_id(0) * tiles_per_group + pl.program_id(1)
    last_step = pl.num_programs(0) * tiles_per_group - 1
    o_copies = functools.partial(_seq_tile_copies, o_hbm, o_buf, o_sem,
                                 tiles_per_group=tiles_per_group, tile_steps=MLP_TT, to_hbm=True)
    x = x_ref[...]
    shift = mod_ref[:, 0:D_MODEL]
    scale = mod_ref[:, D_MODEL:2 * D_MODEL]
    gate = mod_ref[:, 2 * D_MODEL:3 * D_MODEL]
    h2 = _modulate(x, shift, scale).astype(_BF16)
    acc = None
    for k in range(D_FF // FF_CHUNK):
        cols = slice(k * FF_CHUNK, (k + 1) * FF_CHUNK)
        f = jnp.square(jnp.maximum(_dot(h2, w_up_ref[:, cols]), 0.0)).astype(_BF16)
        part = _dot(f, w_down_ref[cols, :])
        acc = part if acc is None else acc + part
    res = ALPHA * x + _per_seq(acc, 1.0 + gate, jnp.multiply)
    out = _layer_norm(res, g_ref[...], b_ref[...])

    @pl.when(step >= 2)
    def _():
        for cp in o_copies(step - 2):
            cp.wait()

    o_buf[step % 2] = out.reshape(MLP_TT, BG, D_MODEL)
    for cp in o_copies(step):
        cp.start()

    @pl.when(step == last_step)
    def _():
        for cp in o_copies(step - 1) + o_copies(step):
            cp.wait()


def _mlp(xt, mod, w_up, w_down, ln_g, ln_b):
    ngroups, nrows, d = xt.shape
    seq = nrows // BG
    return pl.pallas_call(
        _mlp_kernel,
        grid=(ngroups, nrows // MLP_ROWS),
        in_specs=[
            pl.BlockSpec((None, MLP_ROWS, d), lambda g, i: (g, i, 0)),
            pl.BlockSpec((None, BG, 3 * d), lambda g, i: (g, 0, 1)),
            _const_spec(w_up.shape),
            _const_spec(w_down.shape),
            _const_spec(ln_g.shape),
            _const_spec(ln_b.shape),
        ],
        out_specs=pl.BlockSpec(memory_space=pl.ANY),
        out_shape=jax.ShapeDtypeStruct((ngroups * BG, seq, d), _F32),
        scratch_shapes=[
            pltpu.VMEM((2, MLP_TT, BG, d), _F32),
            pltpu.SemaphoreType.DMA((2,)),
        ],
        compiler_params=pltpu.CompilerParams(
            dimension_semantics=("arbitrary", "arbitrary"),
            vmem_limit_bytes=V7X_VMEM_LIMIT,
        ),
        name="channel_mlp",
    )(xt, mod, w_up, w_down, ln_g, ln_b)


def kernel(x, c, w_ada, b_ada, w_in, b_in, w_conv, b_conv, w_rg_a, b_rg_a, w_rg_x, b_rg_x, lru_lambda,
           w_sp, b_sp, ln_v_g, ln_v_b, w_o_lru, w_o_sgu, w_out, ln1_g, ln1_b, w_up, w_down, ln2_g, ln2_b):
    batch, seq, d = x.shape
    depth = w_in.shape[0]
    ngroups = batch // BG
    row = lambda a: a.reshape(1, -1)

    assert batch % BG == 0 and seq % MLP_TT == 0 and (batch // BG) * (seq // MLP_TT) >= 2
    for l in range(depth):
        mod = _ada(c, w_ada[l], b_ada[l]).reshape(ngroups, BG, 6 * d)
        w_rg = jnp.concatenate([w_rg_a[l], w_rg_x[l]], axis=-1).astype(_BF16)
        b_rg = jnp.stack([b_rg_a[l], b_rg_x[l]])
        b_sp_lanes = jnp.broadcast_to(b_sp[l][:, :, None], (SGU_GROUPS, SGU_BLOCK, SGU_GROUP_DIM))
        xt = _mixer(x, mod, w_in[l].astype(_BF16), row(b_in[l]), w_conv[l], row(b_conv[l]), w_rg, b_rg,
                    row(lru_lambda[l]), w_sp[l], b_sp_lanes, row(ln_v_g[l]), row(ln_v_b[l]),
                    w_o_lru[l].astype(_BF16), w_o_sgu[l].astype(_BF16), w_out[l].astype(_BF16),
                    row(ln1_g[l]), row(ln1_b[l]))
        x = _mlp(xt, mod, w_up[l].astype(_BF16), w_down[l].astype(_BF16), row(ln2_g[l]), row(ln2_b[l]))
    return x
```

```python
import functools

import jax
import jax.numpy as jnp
import numpy as np
from jax import lax
from jax.experimental import pallas as pl
from jax.experimental.pallas import tpu as pltpu

D_MODEL = 1024
LRU_WIDTH = 1280
LRU_HEADS = 10
LRU_HEAD_DIM = LRU_WIDTH // LRU_HEADS
CONV_WIDTH = 4
LRU_C = 8.0
SGU_WIDTH = 768
SGU_GROUPS = 6
SGU_GROUP_DIM = SGU_WIDTH // SGU_GROUPS
SGU_BLOCK = 128
CHUNK = 64
D_FF = 4 * D_MODEL
D_IN = 2 * LRU_WIDTH + 2 * SGU_WIDTH + 2 * D_MODEL
ALPHA = 2.0 ** 0.25
LN_EPS = 1e-5

_O_XLRU, _O_GLRU, _O_U, _O_V, _O_GA, _O_GB = 0, 1280, 2560, 3328, 4096, 5120
_RAW_CHUNKS = (((1280, 1536),), ((1536, 1792),), ((1792, 2048),), ((2048, 2304), (2304, 2560)),
               ((2560, 2816), (2816, 3072)))

V7X_F32_SUBLANES = 8
BG = V7X_F32_SUBLANES
TT = CHUNK
ROWS = TT * BG
HALO = (CONV_WIDTH - 1) * BG
MLP_ROWS = 1024
MLP_TT = MLP_ROWS // BG
FF_CHUNK = 1024
V7X_VMEM_LIMIT = 60 * 1024 * 1024

_F32 = jnp.float32
_BF16 = jnp.bfloat16
_GELU_C = np.sqrt(2.0 / np.pi).astype(np.float32)


def _sigmoid(x):
    return 0.5 * jnp.tanh(0.5 * x) + 0.5


def _gelu(x):
    return x * (0.5 * (1.0 + jnp.tanh(_GELU_C * (x + 0.044715 * (x * x * x)))))


def _layer_norm(x, g, b):
    mu = jnp.mean(x, axis=-1, keepdims=True)
    xc = x - mu
    var = jnp.mean(xc * xc, axis=-1, keepdims=True)
    return xc * lax.rsqrt(var + LN_EPS) * g + b


def _per_seq(x, m, op):
    rows, d = x.shape
    return op(x.reshape(rows // BG, BG, d), m[None]).reshape(rows, d)


def _modulate(x, shift, scale):
    return _per_seq(_per_seq(x, 1.0 + scale, jnp.multiply), shift, jnp.add)


def _dot(a, b):
    return jnp.dot(a, b, preferred_element_type=_F32)


def _ada_kernel(c_ref, w_ref, b_ref, o_ref):
    c = c_ref[...]
    c_act = (c * _sigmoid(c)).astype(_BF16)
    o_ref[...] = _dot(c_act, w_ref[...].astype(_BF16)) + b_ref[...]


def _ada(c, w_ada, b_ada):
    b, d = c.shape
    n = w_ada.shape[1]
    bn = D_MODEL
    return pl.pallas_call(
        _ada_kernel,
        grid=(n // bn,),
        in_specs=[
            pl.BlockSpec((b, d), lambda j: (0, 0)),
            pl.BlockSpec((d, bn), lambda j: (0, j)),
            pl.BlockSpec((1, bn), lambda j: (0, j)),
        ],
        out_specs=pl.BlockSpec((b, bn), lambda j: (0, j)),
        out_shape=jax.ShapeDtypeStruct((b, n), _F32),
        name="ada_mod",
    )(c, w_ada, b_ada.reshape(1, n))


def _seq_tile_copies(hbm_ref, buf_ref, sem_ref, step, tiles_per_group, tile_steps, to_hbm):
    g = step // tiles_per_group
    t0 = (step % tiles_per_group) * tile_steps
    slot = step % 2
    copies = []
    for b in range(BG):
        hbm = hbm_ref.at[g * BG + b, pl.ds(t0, tile_steps), :]
        buf = buf_ref.at[slot, :, b, :]
        src, dst = (buf, hbm) if to_hbm else (hbm, buf)
        copies.append(pltpu.make_async_copy(src, dst, sem_ref.at[slot]))
    return copies


def _mixer_kernel(x_hbm, mod_ref, w_in_ref, b_in_ref, w_conv_ref, b_conv_ref, w_rg_ref, b_rg_ref,
                  lam_ref, w_sp_ref, b_sp_ref, lnv_g_ref, lnv_b_ref, w_ol_ref, w_os_ref, w_out_ref,
                  ln1_g_ref, ln1_b_ref, o_ref, x_buf, x_sem, xl_s, a_s, h_s, hc_s, v_s, m_s, raw_s):
    i = pl.program_id(1)
    tiles_per_group = pl.num_programs(1)
    step = pl.program_id(0) * tiles_per_group + i
    last_step = pl.num_programs(0) * tiles_per_group - 1
    x_copies = functools.partial(_seq_tile_copies, x_hbm, x_buf, x_sem,
                                 tiles_per_group=tiles_per_group, tile_steps=TT, to_hbm=False)

    @pl.when(step == 0)
    def _():
        for cp in x_copies(step):
            cp.start()

    @pl.when(step < last_step)
    def _():
        for cp in x_copies(step + 1):
            cp.start()

    for cp in x_copies(step):
        cp.wait()

    @pl.when(i == 0)
    def _():
        xl_s[0:HALO, :] = jnp.zeros((HALO, LRU_WIDTH), _F32)
        hc_s[...] = jnp.zeros((BG, LRU_WIDTH), _F32)
        v_s[1] = jnp.zeros((SGU_GROUPS, ROWS, SGU_GROUP_DIM), _F32)

    x = x_buf[step % 2].reshape(ROWS, D_MODEL)
    shift = mod_ref[:, 0:D_MODEL]
    scale = mod_ref[:, D_MODEL:2 * D_MODEL]
    gate = mod_ref[:, 2 * D_MODEL:3 * D_MODEL]
    hm = _modulate(x, shift, scale).astype(_BF16)

    def proj(lo, hi):
        return _dot(hm, w_in_ref[:, lo:hi]) + b_in_ref[:, lo:hi]

    def proj_raw(lo, hi):
        raw_s[:, lo - _O_GLRU:hi - _O_GLRU] = proj(lo, hi)

    def raw(lo, hi):
        return raw_s[:, lo - _O_GLRU:hi - _O_GLRU]

    z = -lam_ref[...]
    c_lam = -LRU_C * (jnp.maximum(z, 0.0) + jnp.log1p(jnp.exp(-jnp.abs(z))))
    n_pairs = LRU_HEADS // 2

    def pair_cols(p):
        return slice(2 * p * LRU_HEAD_DIM, (2 * p + 2) * LRU_HEAD_DIM)

    def xl_proj(p):
        cols = pair_cols(p)
        xl_s[HALO:HALO + ROWS, cols] = proj(cols.start, cols.stop)

    def conv(p):
        cols = pair_cols(p)
        xc = b_conv_ref[:, cols] + w_conv_ref[0:1, cols] * xl_s[0:ROWS, cols]
        for k in range(1, CONV_WIDTH):
            xc = xc + w_conv_ref[k:k + 1, cols] * xl_s[k * BG:k * BG + ROWS, cols]
        xl_s[0:HALO, cols] = xl_s[ROWS:ROWS + HALO, cols]
        return xc

    def gates(p, xc):
        cols = pair_cols(p)
        xcb = xc.astype(_BF16)
        for j in range(2):
            sl = slice(cols.start + j * LRU_HEAD_DIM, cols.start + (j + 1) * LRU_HEAD_DIM)
            loc = slice(j * LRU_HEAD_DIM, (j + 1) * LRU_HEAD_DIM)
            ri = _dot(xcb[:, loc], w_rg_ref[2 * p + j])
            r = _sigmoid(ri[:, 0:LRU_HEAD_DIM] + b_rg_ref[0:1, sl])
            ig = _sigmoid(ri[:, LRU_HEAD_DIM:] + b_rg_ref[1:2, sl])
            a = jnp.exp(c_lam[:, sl] * r)
            m2 = 1.0 - a * a
            mult = jnp.where(m2 > 0.0, m2 * lax.rsqrt(m2), 0.0)
            a_s[:, sl] = a
            h_s[:, sl] = mult * (ig * xc[:, loc])

    def scan(p):
        cols = pair_cols(p)
        hprev = hc_s[:, cols]
        for t in range(TT):
            rows = slice(t * BG, (t + 1) * BG)
            hprev = a_s[rows, cols] * hprev + h_s[rows, cols]
            h_s[rows, cols] = hprev
        hc_s[:, cols] = hprev

    parity = i % 2
    odd = parity == 1
    v = _layer_norm(_gelu(proj(_O_V, _O_GA)), lnv_g_ref[...], lnv_b_ref[...])
    for g in range(SGU_GROUPS):
        v_s[parity, g] = v[:, g * SGU_GROUP_DIM:(g + 1) * SGU_GROUP_DIM]

    xl_proj(0)
    xl_proj(1)
    for p in range(n_pairs):
        xc = conv(p)
        if p + 2 < n_pairs:
            xl_proj(p + 2)
        for chunk in _RAW_CHUNKS[p]:
            proj_raw(*chunk)
        gates(p, xc)
        scan(p)


    def seq_rows(b):
        return pl.ds(b, TT, stride=BG)

    w_rows = pl.ds(pl.multiple_of(parity * CHUNK, CHUNK), CHUNK)
    col_chunk = lax.broadcasted_iota(jnp.int32, (CHUNK, SGU_BLOCK), 1) // CHUNK
    for g in range(SGU_GROUPS):
        w = jnp.where(col_chunk <= parity, w_sp_ref[g, w_rows, :], 0.0).astype(_BF16)
        vs = [jnp.concatenate([v_s[0, g, seq_rows(b), :],
                               jnp.where(odd, v_s[1, g, seq_rows(b), :], 0.0)], axis=0)
              for b in range(BG)]
        out = _dot(w, jnp.concatenate(vs, axis=1).astype(_BF16))
        bias = b_sp_ref[g, w_rows, :]
        for b in range(BG):
            m_s[g, seq_rows(b), :] = out[:, b * SGU_GROUP_DIM:(b + 1) * SGU_GROUP_DIM] + bias

    proj_raw(_O_U + 512, _O_V)
    ya_in = (h_s[...] * _gelu(raw(_O_GLRU, _O_U))).astype(_BF16)
    gate_a = _sigmoid(proj(_O_GA, _O_GB))
    y_a = _dot(ya_in, w_ol_ref[...])
    gate_b = _sigmoid(proj(_O_GB, D_IN))
    mixed = jnp.concatenate([m_s[g] for g in range(SGU_GROUPS)], axis=1)
    y_b = _dot((_gelu(raw(_O_U, _O_V)) * mixed).astype(_BF16), w_os_ref[...])

    merged = gate_a * y_a + gate_b * y_b
    mix = _dot(merged.astype(_BF16), w_out_ref[...])
    res = ALPHA * x + _per_seq(mix, 1.0 + gate, jnp.multiply)
    o_ref[...] = _layer_norm(res, ln1_g_ref[...], ln1_b_ref[...])


def _const_spec(shape):
    nd = len(shape)
    return pl.BlockSpec(shape, lambda *_: (0,) * nd, pipeline_mode=pl.Buffered(1))


def _mixer(x, mod, w_in, b_in, w_conv, b_conv, w_rg, b_rg, lam, w_sp, b_sp, lnv_g, lnv_b,
           w_ol, w_os, w_out, ln1_g, ln1_b):
    batch, seq, d = x.shape
    ngroups = batch // BG
    consts = (w_in, b_in, w_conv, b_conv, w_rg, b_rg, lam, w_sp, b_sp, lnv_g, lnv_b,
              w_ol, w_os, w_out, ln1_g, ln1_b)
    return pl.pallas_call(
        _mixer_kernel,
        grid=(ngroups, seq // TT),
        in_specs=[
            pl.BlockSpec(memory_space=pl.ANY),
            pl.BlockSpec((None, BG, 3 * d), lambda g, i: (g, 0, 0)),
        ] + [_const_spec(a.shape) for a in consts],
        out_specs=pl.BlockSpec((None, ROWS, d), lambda g, i: (g, i, 0)),
        out_shape=jax.ShapeDtypeStruct((ngroups, seq * BG, d), _F32),
        scratch_shapes=[
            pltpu.VMEM((2, TT, BG, d), _F32),
            pltpu.SemaphoreType.DMA((2,)),
            pltpu.VMEM((HALO + ROWS, LRU_WIDTH), _F32),
            pltpu.VMEM((ROWS, LRU_WIDTH), _F32),
            pltpu.VMEM((ROWS, LRU_WIDTH), _F32),
            pltpu.VMEM((BG, LRU_WIDTH), _F32),
            pltpu.VMEM((2, SGU_GROUPS, ROWS, SGU_GROUP_DIM), _F32),
            pltpu.VMEM((SGU_GROUPS, ROWS, SGU_GROUP_DIM), _F32),
            pltpu.VMEM((ROWS, _O_V - _O_GLRU), _F32),
        ],
        compiler_params=pltpu.CompilerParams(
            dimension_semantics=("arbitrary", "arbitrary"),
            vmem_limit_bytes=V7X_VMEM_LIMIT,
        ),
        name="token_mixer",
    )(x, mod, *consts)


def _mlp_kernel(x_ref, mod_ref, w_up_ref, w_down_ref, g_ref, b_ref, o_hbm, o_buf, o_sem):
    tiles_per_group = pl.num_programs(1)
    step = pl.program_id(0) * tiles_per_group + pl.program_id(1)
    last_step = pl.num_programs(0) * tiles_per_group - 1
    o_copies = functools.partial(_seq_tile_copies, o_hbm, o_buf, o_sem,
                                 tiles_per_group=tiles_per_group, tile_steps=MLP_TT, to_hbm=True)
    x = x_ref[...]
    shift = mod_ref[:, 0:D_MODEL]
    scale = mod_ref[:, D_MODEL:2 * D_MODEL]
    gate = mod_ref[:, 2 * D_MODEL:3 * D_MODEL]
    h2 = _modulate(x, shift, scale).astype(_BF16)
    acc = None
    for k in range(D_FF // FF_CHUNK):
        cols = slice(k * FF_CHUNK, (k + 1) * FF_CHUNK)
        f = jnp.square(jnp.maximum(_dot(h2, w_up_ref[:, cols]), 0.0)).astype(_BF16)
        part = _dot(f, w_down_ref[cols, :])
        acc = part if acc is None else acc + part
    res = ALPHA * x + _per_seq(acc, 1.0 + gate, jnp.multiply)
    out = _layer_norm(res, g_ref[...], b_ref[...])

    @pl.when(step >= 2)
    def _():
        for cp in o_copies(step - 2):
            cp.wait()

    o_buf[step % 2] = out.reshape(MLP_TT, BG, D_MODEL)
    for cp in o_copies(step):
        cp.start()

    @pl.when(step == last_step)
    def _():
        for cp in o_copies(step - 1) + o_copies(step):
            cp.wait()


def _mlp(xt, mod, w_up, w_down, ln_g, ln_b):
    ngroups, nrows, d = xt.shape
    seq = nrows // BG
    return pl.pallas_call(
        _mlp_kernel,
        grid=(ngroups, nrows // MLP_ROWS),
        in_specs=[
            pl.BlockSpec((None, MLP_ROWS, d), lambda g, i: (g, i, 0)),
            pl.BlockSpec((None, BG, 3 * d), lambda g, i: (g, 0, 1)),
            _const_spec(w_up.shape),
            _const_spec(w_down.shape),
            _const_spec(ln_g.shape),
            _const_spec(ln_b.shape),
        ],
        out_specs=pl.BlockSpec(memory_space=pl.ANY),
        out_shape=jax.ShapeDtypeStruct((ngroups * BG, seq, d), _F32),
        scratch_shapes=[
            pltpu.VMEM((2, MLP_TT, BG, d), _F32),
            pltpu.SemaphoreType.DMA((2,)),
        ],
        compiler_params=pltpu.CompilerParams(
            dimension_semantics=("arbitrary", "arbitrary"),
            vmem_limit_bytes=V7X_VMEM_LIMIT,
        ),
        name="channel_mlp",
    )(xt, mod, w_up, w_down, ln_g, ln_b)


def kernel(x, c, w_ada, b_ada, w_in, b_in, w_conv, b_conv, w_rg_a, b_rg_a, w_rg_x, b_rg_x, lru_lambda,
           w_sp, b_sp, ln_v_g, ln_v_b, w_o_lru, w_o_sgu, w_out, ln1_g, ln1_b, w_up, w_down, ln2_g, ln2_b):
    batch, seq, d = x.shape
    depth = w_in.shape[0]
    ngroups = batch // BG
    row = lambda a: a.reshape(1, -1)

    assert batch % BG == 0 and seq % MLP_TT == 0 and (batch // BG) * (seq // MLP_TT) >= 2
    for l in range(depth):
        mod = _ada(c, w_ada[l], b_ada[l]).reshape(ngroups, BG, 6 * d)
        w_rg = jnp.concatenate([w_rg_a[l], w_rg_x[l]], axis=-1).astype(_BF16)
        b_rg = jnp.stack([b_rg_a[l], b_rg_x[l]])
        b_sp_lanes = jnp.broadcast_to(b_sp[l][:, :, None], (SGU_GROUPS, SGU_BLOCK, SGU_GROUP_DIM))
        xt = _mixer(x, mod, w_in[l].astype(_BF16), row(b_in[l]), w_conv[l], row(b_conv[l]), w_rg, b_rg,
                    row(lru_lambda[l]), w_sp[l], b_sp_lanes, row(ln_v_g[l]), row(ln_v_b[l]),
                    w_o_lru[l].astype(_BF16), w_o_sgu[l].astype(_BF16), w_out[l].astype(_BF16),
                    row(ln1_g[l]), row(ln1_b[l]))
        x = _mlp(xt, mod, w_up[l].astype(_BF16), w_down[l].astype(_BF16), row(ln2_g[l]), row(ln2_b[l]))
    return x
```

```python
import functools

import jax
import jax.numpy as jnp
import numpy as np
from jax import lax
from jax.experimental import pallas as pl
from jax.experimental.pallas import tpu as pltpu

D_MODEL = 1024
LRU_WIDTH = 1280
LRU_HEADS = 10
LRU_HEAD_DIM = LRU_WIDTH // LRU_HEADS
CONV_WIDTH = 4
LRU_C = 8.0
SGU_WIDTH = 768
SGU_GROUPS = 6
SGU_GROUP_DIM = SGU_WIDTH // SGU_GROUPS
SGU_BLOCK = 128
CHUNK = 64
D_FF = 4 * D_MODEL
D_IN = 2 * LRU_WIDTH + 2 * SGU_WIDTH + 2 * D_MODEL
ALPHA = 2.0 ** 0.25
LN_EPS = 1e-5

_O_XLRU, _O_GLRU, _O_U, _O_V, _O_GA, _O_GB = 0, 1280, 2560, 3328, 4096, 5120
_RAW_CHUNKS = (((1280, 1536),), ((1536, 1792),), ((1792, 2048),), ((2048, 2304), (2304, 2560)),
               ((2560, 2816), (2816, 3072)))

V7X_F32_SUBLANES = 8
BG = V7X_F32_SUBLANES
TT = CHUNK
ROWS = TT * BG
HALO = (CONV_WIDTH - 1) * BG
MLP_ROWS = 1024
MLP_TT = MLP_ROWS // BG
MLP_SUB_ROWS = (256, 256, 256, 256)
FF_CHUNK = 1024
V7X_VMEM_LIMIT = 60 * 1024 * 1024

_F32 = jnp.float32
_BF16 = jnp.bfloat16
_GELU_C = np.sqrt(2.0 / np.pi).astype(np.float32)
_GELU_CK = _GELU_C * np.float32(0.044715)
_TINY = 1e-30


def _sigmoid(x):
    return _sigmoid_of_half(0.5 * x)


def _sigmoid_of_half(half_x):
    return 0.5 * jnp.tanh(half_x) + 0.5


def _gelu(x):
    half_x = 0.5 * x
    return half_x * jnp.tanh(x * (_GELU_C + _GELU_CK * (x * x))) + half_x


def _layer_norm(x, g, b):
    mu = jnp.mean(x, axis=-1, keepdims=True)
    xc = x - mu
    var = jnp.mean(xc * xc, axis=-1, keepdims=True)
    return xc * lax.rsqrt(var + LN_EPS) * g + b


def _per_seq(x, m, op):
    rows, d = x.shape
    return op(x.reshape(rows // BG, BG, d), m[None]).reshape(rows, d)


def _modulate(x, shift, scale):
    return _per_seq(_per_seq(x, 1.0 + scale, jnp.multiply), shift, jnp.add)


def _dot(a, b):
    return jnp.dot(a, b, preferred_element_type=_F32)


def _ada_kernel(c_ref, w_ref, b_ref, o_ref):
    c = c_ref[...]
    c_act = (c * _sigmoid(c)).astype(_BF16)
    o_ref[...] = _dot(c_act, w_ref[...].astype(_BF16)) + b_ref[...]


def _ada(c, w_ada, b_ada):
    b, d = c.shape
    n = w_ada.shape[1]
    bn = D_MODEL
    return pl.pallas_call(
        _ada_kernel,
        grid=(n // bn,),
        in_specs=[
            pl.BlockSpec((b, d), lambda j: (0, 0)),
            pl.BlockSpec((d, bn), lambda j: (0, j)),
            pl.BlockSpec((1, bn), lambda j: (0, j)),
        ],
        out_specs=pl.BlockSpec((b, bn), lambda j: (0, j)),
        out_shape=jax.ShapeDtypeStruct((b, n), _F32),
        name="ada_mod",
    )(c, w_ada, b_ada.reshape(1, n))


def _seq_tile_copies(hbm_ref, buf_ref, sem_ref, step, tiles_per_group, tile_steps, to_hbm):
    g = step // tiles_per_group
    t0 = (step % tiles_per_group) * tile_steps
    slot = step % 2
    copies = []
    for b in range(BG):
        hbm = hbm_ref.at[g * BG + b, pl.ds(t0, tile_steps), :]
        buf = buf_ref.at[slot, :, b, :]
        src, dst = (buf, hbm) if to_hbm else (hbm, buf)
        copies.append(pltpu.make_async_copy(src, dst, sem_ref.at[slot]))
    return copies


def _mixer_kernel(x_hbm, mod_ref, w_in_ref, b_in_ref, w_conv_ref, b_conv_ref, w_rg_ref, b_rg_ref,
                  lam_ref, w_sp_ref, b_sp_ref, lnv_g_ref, lnv_b_ref, w_ol_ref, w_os_ref, w_out_ref,
                  o_ref, x_buf, x_sem, xl_s, a_s, h_s, hc_s, v_s, m_s, raw_s):
    i = pl.program_id(1)
    steps_per_group = pl.num_programs(1)
    step = pl.program_id(0) * steps_per_group + i
    last_step = pl.num_programs(0) * steps_per_group - 1
    x_copies = functools.partial(_seq_tile_copies, x_hbm, x_buf, x_sem,
                                 tiles_per_group=steps_per_group, tile_steps=SGU_BLOCK, to_hbm=False)

    @pl.when(step == 0)
    def _():
        for cp in x_copies(step):
            cp.start()

    @pl.when(step < last_step)
    def _():
        for cp in x_copies(step + 1):
            cp.start()

    for cp in x_copies(step):
        cp.wait()

    @pl.when(i == 0)
    def _():
        xl_s[0:HALO, :] = jnp.zeros((HALO, LRU_WIDTH), _F32)
        hc_s[...] = jnp.zeros((BG, LRU_WIDTH), _F32)

    shift = mod_ref[:, 0:D_MODEL]
    scale = mod_ref[:, D_MODEL:2 * D_MODEL]
    gate = mod_ref[:, 2 * D_MODEL:3 * D_MODEL]
    z = -lam_ref[...]
    c_lam = -LRU_C * (jnp.maximum(z, 0.0) + jnp.log1p(jnp.exp(-jnp.abs(z))))
    n_pairs = LRU_HEADS // 2

    def proj(hm, lo, hi):
        return _dot(hm, w_in_ref[:, lo:hi]) + b_in_ref[:, lo:hi]

    def proj_raw(hm, lo, hi):
        raw_s[:, lo - _O_GLRU:hi - _O_GLRU] = proj(hm, lo, hi)

    def raw(lo, hi):
        return raw_s[:, lo - _O_GLRU:hi - _O_GLRU]

    def pair_cols(p):
        return slice(2 * p * LRU_HEAD_DIM, (2 * p + 2) * LRU_HEAD_DIM)

    def xl_proj(hm, p):
        cols = pair_cols(p)
        xl_s[HALO:HALO + ROWS, cols] = proj(hm, cols.start, cols.stop)

    def conv(p):
        cols = pair_cols(p)
        xc = b_conv_ref[:, cols] + w_conv_ref[0:1, cols] * xl_s[0:ROWS, cols]
        for k in range(1, CONV_WIDTH):
            xc = xc + w_conv_ref[k:k + 1, cols] * xl_s[k * BG:k * BG + ROWS, cols]
        xl_s[0:HALO, cols] = xl_s[ROWS:ROWS + HALO, cols]
        return xc

    def gates(p, xc):
        cols = pair_cols(p)
        xcb = xc.astype(_BF16)
        for j in range(2):
            sl = slice(cols.start + j * LRU_HEAD_DIM, cols.start + (j + 1) * LRU_HEAD_DIM)
            loc = slice(j * LRU_HEAD_DIM, (j + 1) * LRU_HEAD_DIM)
            ri = _dot(xcb[:, loc], w_rg_ref[2 * p + j])
            r = _sigmoid_of_half(ri[:, 0:LRU_HEAD_DIM] + b_rg_ref[0:1, sl])
            ig = _sigmoid_of_half(ri[:, LRU_HEAD_DIM:] + b_rg_ref[1:2, sl])
            a = jnp.exp(c_lam[:, sl] * r)
            m2 = 1.0 - a * a
            mult = m2 * lax.rsqrt(jnp.maximum(m2, _TINY))
            a_s[:, sl] = a
            h_s[:, sl] = mult * (ig * xc[:, loc])

    def scan(p):
        cols = pair_cols(p)
        hprev = hc_s[:, cols]
        for t in range(TT):
            rows = slice(t * BG, (t + 1) * BG)
            hprev = a_s[rows, cols] * hprev + h_s[rows, cols]
            h_s[rows, cols] = hprev
        hc_s[:, cols] = hprev

    def seq_rows(b):
        return pl.ds(b, TT, stride=BG)

    def head(chunk):
        x = x_buf[step % 2, chunk * TT:(chunk + 1) * TT].reshape(ROWS, D_MODEL)
        hm = _modulate(x, shift, scale).astype(_BF16)
        v = _layer_norm(_gelu(proj(hm, _O_V, _O_GA)), lnv_g_ref[...], lnv_b_ref[...])
        for g in range(SGU_GROUPS):
            v_s[chunk, g] = v[:, g * SGU_GROUP_DIM:(g + 1) * SGU_GROUP_DIM]
        xl_proj(hm, 0)
        xl_proj(hm, 1)
        return x, hm

    def recurrent_branch(hm):
        for p in range(n_pairs):
            xc = conv(p)
            if p + 2 < n_pairs:
                xl_proj(hm, p + 2)
            for chunk_cols in _RAW_CHUNKS[p]:
                proj_raw(hm, *chunk_cols)
            gates(p, xc)
            scan(p)

    def spatial_mix(chunk, g):
        if chunk == 0:
            w = w_sp_ref[g, 0:CHUNK, 0:CHUNK]
            vs = [v_s[0, g, seq_rows(b), :] for b in range(BG)]
        else:
            w = w_sp_ref[g, CHUNK:SGU_BLOCK, :]
            vs = [jnp.concatenate([v_s[0, g, seq_rows(b), :], v_s[1, g, seq_rows(b), :]], axis=0)
                  for b in range(BG)]
        out = _dot(w.astype(_BF16), jnp.concatenate(vs, axis=1).astype(_BF16))
        bias = b_sp_ref[g, chunk * CHUNK:(chunk + 1) * CHUNK, :]
        for b in range(BG):
            m_s[g, seq_rows(b), :] = out[:, b * SGU_GROUP_DIM:(b + 1) * SGU_GROUP_DIM] + bias

    def gating_branch(chunk, hm):
        gate_a_parts = []
        for g in range(SGU_GROUPS):
            spatial_mix(chunk, g)
            if g == 0:
                proj_raw(hm, _O_U + 512, _O_V)
            elif g <= 4:
                lo = _O_GA + (g - 1) * 256
                gate_a_parts.append(_sigmoid_of_half(proj(hm, lo, lo + 256)))
        gate_a = jnp.concatenate(gate_a_parts, axis=1)
        ya_in = (h_s[...] * _gelu(raw(_O_GLRU, _O_U))).astype(_BF16)
        return gate_a, ya_in

    def tail(chunk, x, hm, gate_a, y_a):
        gate_b = _sigmoid_of_half(proj(hm, _O_GB, D_IN))
        mixed = jnp.concatenate([m_s[g] for g in range(SGU_GROUPS)], axis=1)
        y_b = _dot((_gelu(raw(_O_U, _O_V)) * mixed).astype(_BF16), w_os_ref[...])
        merged = gate_a * y_a + gate_b * y_b
        mix = _dot(merged.astype(_BF16), w_out_ref[...])
        o_ref[chunk * ROWS:(chunk + 1) * ROWS, :] = ALPHA * x + _per_seq(mix, 1.0 + gate, jnp.multiply)

    x0, hm0 = head(0)
    recurrent_branch(hm0)
    gate_a0, ya_in0 = gating_branch(0, hm0)
    y_a0 = _dot(ya_in0, w_ol_ref[...])
    x1, hm1 = head(1)
    tail(0, x0, hm0, gate_a0, y_a0)
    recurrent_branch(hm1)
    gate_a1, ya_in1 = gating_branch(1, hm1)
    tail(1, x1, hm1, gate_a1, _dot(ya_in1, w_ol_ref[...]))


def _const_spec(shape):
    nd = len(shape)
    return pl.BlockSpec(shape, lambda *_: (0,) * nd, pipeline_mode=pl.Buffered(1))


def _mixer(x, mod, w_in, b_in, w_conv, b_conv, w_rg, b_rg, lam, w_sp, b_sp, lnv_g, lnv_b,
           w_ol, w_os, w_out):
    batch, seq, d = x.shape
    ngroups = batch // BG
    consts = (w_in, b_in, w_conv, b_conv, w_rg, b_rg, lam, w_sp, b_sp, lnv_g, lnv_b,
              w_ol, w_os, w_out)
    return pl.pallas_call(
        _mixer_kernel,
        grid=(ngroups, seq // SGU_BLOCK),
        in_specs=[
            pl.BlockSpec(memory_space=pl.ANY),
            pl.BlockSpec((None, BG, 3 * d), lambda g, i: (g, 0, 0)),
        ] + [_const_spec(a.shape) for a in consts],
        out_specs=pl.BlockSpec((None, 2 * ROWS, d), lambda g, i: (g, i, 0)),
        out_shape=jax.ShapeDtypeStruct((ngroups, seq * BG, d), _F32),
        scratch_shapes=[
            pltpu.VMEM((2, SGU_BLOCK, BG, d), _F32),
            pltpu.SemaphoreType.DMA((2,)),
            pltpu.VMEM((HALO + ROWS, LRU_WIDTH), _F32),
            pltpu.VMEM((ROWS, LRU_WIDTH), _F32),
            pltpu.VMEM((ROWS, LRU_WIDTH), _F32),
            pltpu.VMEM((BG, LRU_WIDTH), _F32),
            pltpu.VMEM((2, SGU_GROUPS, ROWS, SGU_GROUP_DIM), _F32),
            pltpu.VMEM((SGU_GROUPS, ROWS, SGU_GROUP_DIM), _F32),
            pltpu.VMEM((ROWS, _O_V - _O_GLRU), _F32),
        ],
        compiler_params=pltpu.CompilerParams(
            dimension_semantics=("arbitrary", "arbitrary"),
            vmem_limit_bytes=V7X_VMEM_LIMIT,
        ),
        name="token_mixer",
    )(x, mod, *consts)


def _mlp_kernel(r_ref, mod_ref, ln1_g_ref, ln1_b_ref, w_up_ref, w_down_ref, ln2_g_ref, ln2_b_ref,
                o_hbm, o_buf, o_sem):
    tiles_per_group = pl.num_programs(1)
    step = pl.program_id(0) * tiles_per_group + pl.program_id(1)
    last_step = pl.num_programs(0) * tiles_per_group - 1
    o_copies = functools.partial(_seq_tile_copies, o_hbm, o_buf, o_sem,
                                 tiles_per_group=tiles_per_group, tile_steps=MLP_TT, to_hbm=True)
    shift = mod_ref[:, 0:D_MODEL]
    scale = mod_ref[:, D_MODEL:2 * D_MODEL]
    gate = mod_ref[:, 2 * D_MODEL:3 * D_MODEL]

    @pl.when(step >= 2)
    def _():
        for cp in o_copies(step - 2):
            cp.wait()

    sub_start = [sum(MLP_SUB_ROWS[:r]) for r in range(len(MLP_SUB_ROWS) + 1)]

    def prep(r):
        rows = slice(sub_start[r], sub_start[r + 1])
        x = _layer_norm(r_ref[rows, :], ln1_g_ref[...], ln1_b_ref[...])
        return x, _modulate(x, shift, scale).astype(_BF16)

    def up(h2, k):
        cols = slice(k * FF_CHUNK, (k + 1) * FF_CHUNK)
        return jnp.square(jnp.maximum(_dot(h2, w_up_ref[:, cols]), 0.0)).astype(_BF16)

    def down(f, k):
        return _dot(f, w_down_ref[k * FF_CHUNK:(k + 1) * FF_CHUNK, :])

    n_sub = len(MLP_SUB_ROWS)
    n_chunks = D_FF // FF_CHUNK
    nxt = prep(0)
    f_first = up(nxt[1], 0)
    for r in range(n_sub):
        x, h2 = nxt
        acc = down(f_first, 0)
        for k in range(1, n_chunks):
            if k == n_chunks - 1 and r + 1 < n_sub:
                nxt = prep(r + 1)
                f_first = up(nxt[1], 0)
            acc = acc + down(up(h2, k), k)
        res = ALPHA * x + _per_seq(acc, 1.0 + gate, jnp.multiply)
        out = _layer_norm(res, ln2_g_ref[...], ln2_b_ref[...])
        steps = slice(sub_start[r] // BG, sub_start[r + 1] // BG)
        o_buf[step % 2, steps] = out.reshape(MLP_SUB_ROWS[r] // BG, BG, D_MODEL)

    for cp in o_copies(step):
        cp.start()

    @pl.when(step == last_step)
    def _():
        for cp in o_copies(step - 1) + o_copies(step):
            cp.wait()


def _mlp(rt, mod, ln1_g, ln1_b, w_up, w_down, ln2_g, ln2_b):
    ngroups, nrows, d = rt.shape
    seq = nrows // BG
    consts = (ln1_g, ln1_b, w_up, w_down, ln2_g, ln2_b)
    return pl.pallas_call(
        _mlp_kernel,
        grid=(ngroups, nrows // MLP_ROWS),
        in_specs=[
            pl.BlockSpec((None, MLP_ROWS, d), lambda g, i: (g, i, 0)),
            pl.BlockSpec((None, BG, 3 * d), lambda g, i: (g, 0, 1)),
        ] + [_const_spec(a.shape) for a in consts],
        out_specs=pl.BlockSpec(memory_space=pl.ANY),
        out_shape=jax.ShapeDtypeStruct((ngroups * BG, seq, d), _F32),
        scratch_shapes=[
            pltpu.VMEM((2, MLP_TT, BG, d), _F32),
            pltpu.SemaphoreType.DMA((2,)),
        ],
        compiler_params=pltpu.CompilerParams(
            dimension_semantics=("arbitrary", "arbitrary"),
            vmem_limit_bytes=V7X_VMEM_LIMIT,
        ),
        name="channel_mlp",
    )(rt, mod, *consts)


def kernel(x, c, w_ada, b_ada, w_in, b_in, w_conv, b_conv, w_rg_a, b_rg_a, w_rg_x, b_rg_x, lru_lambda,
           w_sp, b_sp, ln_v_g, ln_v_b, w_o_lru, w_o_sgu, w_out, ln1_g, ln1_b, w_up, w_down, ln2_g, ln2_b):
    batch, seq, d = x.shape
    depth = w_in.shape[0]
    ngroups = batch // BG
    row = lambda a: a.reshape(1, -1)

    assert batch % BG == 0 and seq % MLP_TT == 0 and (batch // BG) * (seq // MLP_TT) >= 2
    for l in range(depth):
        mod = _ada(c, w_ada[l], b_ada[l]).reshape(ngroups, BG, 6 * d)
        w_rg = (0.5 * jnp.concatenate([w_rg_a[l], w_rg_x[l]], axis=-1)).astype(_BF16)
        b_rg = 0.5 * jnp.stack([b_rg_a[l], b_rg_x[l]])
        gate_scale = jnp.where(jnp.arange(D_IN) >= _O_GA, 0.5, 1.0).astype(_F32)
        b_sp_lanes = jnp.broadcast_to(b_sp[l][:, :, None], (SGU_GROUPS, SGU_BLOCK, SGU_GROUP_DIM))
        xt = _mixer(x, mod, (w_in[l] * gate_scale).astype(_BF16), row(b_in[l] * gate_scale),
                    w_conv[l], row(b_conv[l]), w_rg, b_rg,
                    row(lru_lambda[l]), w_sp[l], b_sp_lanes, row(ln_v_g[l]), row(ln_v_b[l]),
                    w_o_lru[l].astype(_BF16), w_o_sgu[l].astype(_BF16), w_out[l].astype(_BF16))
        x = _mlp(xt, mod, row(ln1_g[l]), row(ln1_b[l]), w_up[l].astype(_BF16), w_down[l].astype(_BF16),
                 row(ln2_g[l]), row(ln2_b[l]))
    return x
```

```python
import functools

import jax
import jax.numpy as jnp
import numpy as np
from jax import lax
from jax.experimental import pallas as pl
from jax.experimental.pallas import tpu as pltpu

D_MODEL = 1024
LRU_WIDTH = 1280
LRU_HEADS = 10
LRU_HEAD_DIM = LRU_WIDTH // LRU_HEADS
CONV_WIDTH = 4
LRU_C = 8.0
SGU_WIDTH = 768
SGU_GROUPS = 6
SGU_GROUP_DIM = SGU_WIDTH // SGU_GROUPS
SGU_BLOCK = 128
CHUNK = 64
D_FF = 4 * D_MODEL
D_IN = 2 * LRU_WIDTH + 2 * SGU_WIDTH + 2 * D_MODEL
ALPHA = 2.0 ** 0.25
LN_EPS = 1e-5

_O_XLRU, _O_GLRU, _O_U, _O_V, _O_GA, _O_GB = 0, 1280, 2560, 3328, 4096, 5120
_RAW_CHUNKS = (((1280, 1536),), ((1536, 1792),), ((1792, 2048),), ((2048, 2304), (2304, 2560)),
               ((2560, 2816), (2816, 3072)))

V7X_F32_SUBLANES = 8
BG = V7X_F32_SUBLANES
TT = CHUNK
ROWS = TT * BG
HALO = (CONV_WIDTH - 1) * BG
MLP_ROWS = 1024
MLP_TT = MLP_ROWS // BG
MLP_SUB_ROWS = (256,) * 4
FF_CHUNK = 1024
V7X_VMEM_LIMIT = 60 * 1024 * 1024

_F32 = jnp.float32
_BF16 = jnp.bfloat16
_GELU_C = np.sqrt(2.0 / np.pi).astype(np.float32)
_GELU_CK = _GELU_C * np.float32(0.044715)
_LOG2_E = np.float32(np.log2(np.e))
_TINY = 1e-30


def _sigmoid(x):
    return _sigmoid_of_half(0.5 * x)


def _sigmoid_of_half(half_x):
    return 0.5 * jnp.tanh(half_x) + 0.5


def _gelu_of_half(h):
    return h * jnp.tanh(h * (2.0 * _GELU_C + (8.0 * _GELU_CK) * (h * h))) + h


def _layer_norm(x, g, b):
    mu = jnp.mean(x, axis=-1, keepdims=True)
    xc = x - mu
    var = jnp.mean(xc * xc, axis=-1, keepdims=True)
    return xc * lax.rsqrt(var + LN_EPS) * g + b


def _per_seq(x, m, op):
    rows, d = x.shape
    return op(x.reshape(rows // BG, BG, d), m[None]).reshape(rows, d)


def _modulate(x, shift, scale):
    return _per_seq(_per_seq(x, 1.0 + scale, jnp.multiply), shift, jnp.add)


def _dot(a, b):
    return jnp.dot(a, b, preferred_element_type=_F32)


def _ada_kernel(c_ref, w_ref, b_ref, o_ref):
    c = c_ref[...]
    c_act = (c * _sigmoid(c)).astype(_BF16)
    o_ref[...] = _dot(c_act, w_ref[...].astype(_BF16)) + b_ref[...]


def _ada(c, w_ada, b_ada):
    b, d = c.shape
    n = w_ada.shape[1]
    bn = D_MODEL
    return pl.pallas_call(
        _ada_kernel,
        grid=(n // bn,),
        in_specs=[
            pl.BlockSpec((b, d), lambda j: (0, 0)),
            pl.BlockSpec((d, bn), lambda j: (0, j)),
            pl.BlockSpec((1, bn), lambda j: (0, j)),
        ],
        out_specs=pl.BlockSpec((b, bn), lambda j: (0, j)),
        out_shape=jax.ShapeDtypeStruct((b, n), _F32),
        name="ada_mod",
    )(c, w_ada, b_ada.reshape(1, n))


def _seq_tile_copies(hbm_ref, buf_ref, sem_ref, step, tiles_per_group, tile_steps, to_hbm):
    g = step // tiles_per_group
    t0 = (step % tiles_per_group) * tile_steps
    slot = step % 2
    copies = []
    for b in range(BG):
        hbm = hbm_ref.at[g * BG + b, pl.ds(t0, tile_steps), :]
        buf = buf_ref.at[slot, :, b, :]
        src, dst = (buf, hbm) if to_hbm else (hbm, buf)
        copies.append(pltpu.make_async_copy(src, dst, sem_ref.at[slot]))
    return copies


def _mixer_kernel(x_hbm, mod_ref, w_in_ref, b_in_ref, w_conv_ref, b_conv_ref, w_rg_ref, b_rg_ref,
                  lam_ref, w_sp_ref, b_sp_ref, lnv_g_ref, lnv_b_ref, w_ol_ref, w_os_ref, w_out_ref,
                  o_ref, x_buf, x_sem, xl_s, a_s, h_s, hc_s, v_s, m_s, raw_s):
    i = pl.program_id(1)
    steps_per_group = pl.num_programs(1)
    step = pl.program_id(0) * steps_per_group + i
    last_step = pl.num_programs(0) * steps_per_group - 1
    x_copies = functools.partial(_seq_tile_copies, x_hbm, x_buf, x_sem,
                                 tiles_per_group=steps_per_group, tile_steps=SGU_BLOCK, to_hbm=False)

    @pl.when(step == 0)
    def _():
        for cp in x_copies(step):
            cp.start()

    @pl.when(step < last_step)
    def _():
        for cp in x_copies(step + 1):
            cp.start()

    for cp in x_copies(step):
        cp.wait()

    @pl.when(i == 0)
    def _():
        xl_s[0:HALO, :] = jnp.zeros((HALO, LRU_WIDTH), _F32)
        hc_s[...] = jnp.zeros((BG, LRU_WIDTH), _F32)

    shift = mod_ref[:, 0:D_MODEL]
    scale = mod_ref[:, D_MODEL:2 * D_MODEL]
    gate = mod_ref[:, 2 * D_MODEL:3 * D_MODEL]
    z = -lam_ref[...]
    c_lam = -LRU_C * (jnp.maximum(z, 0.0) + jnp.log1p(jnp.exp(-jnp.abs(z))))
    half_c_log2e = (0.5 * _LOG2_E) * c_lam
    n_pairs = LRU_HEADS // 2

    def proj(hm, lo, hi):
        return _dot(hm, w_in_ref[:, lo:hi]) + b_in_ref[:, lo:hi]

    def proj_raw(hm, lo, hi):
        raw_s[:, lo - _O_GLRU:hi - _O_GLRU] = proj(hm, lo, hi)

    def raw(lo, hi):
        return raw_s[:, lo - _O_GLRU:hi - _O_GLRU]

    def pair_cols(p):
        return slice(2 * p * LRU_HEAD_DIM, (2 * p + 2) * LRU_HEAD_DIM)

    def xl_proj(hm, p):
        cols = pair_cols(p)
        xl_s[HALO:HALO + ROWS, cols] = proj(hm, cols.start, cols.stop)

    def conv(p):
        cols = pair_cols(p)
        xc = b_conv_ref[:, cols] + w_conv_ref[0:1, cols] * xl_s[0:ROWS, cols]
        for k in range(1, CONV_WIDTH):
            xc = xc + w_conv_ref[k:k + 1, cols] * xl_s[k * BG:k * BG + ROWS, cols]
        xl_s[0:HALO, cols] = xl_s[ROWS:ROWS + HALO, cols]
        return xc

    def gates(p, xc):
        cols = pair_cols(p)
        xcb = xc.astype(_BF16)
        for j in range(2):
            sl = slice(cols.start + j * LRU_HEAD_DIM, cols.start + (j + 1) * LRU_HEAD_DIM)
            loc = slice(j * LRU_HEAD_DIM, (j + 1) * LRU_HEAD_DIM)
            ri = _dot(xcb[:, loc], w_rg_ref[2 * p + j])
            tanh_r = jnp.tanh(ri[:, 0:LRU_HEAD_DIM] + b_rg_ref[0:1, sl])
            ig = _sigmoid_of_half(ri[:, LRU_HEAD_DIM:] + b_rg_ref[1:2, sl])
            a = jnp.exp2(half_c_log2e[:, sl] * tanh_r + half_c_log2e[:, sl])
            m2 = 1.0 - a * a
            mult = m2 * lax.rsqrt(jnp.maximum(m2, _TINY))
            a_s[:, sl] = a
            h_s[:, sl] = mult * (ig * xc[:, loc])

    def scan(p):
        cols = pair_cols(p)
        hprev = hc_s[:, cols]
        for t in range(TT):
            rows = slice(t * BG, (t + 1) * BG)
            hprev = a_s[rows, cols] * hprev + h_s[rows, cols]
            h_s[rows, cols] = hprev
        hc_s[:, cols] = hprev

    def seq_rows(b):
        return pl.ds(b, TT, stride=BG)

    def head(chunk):
        x = x_buf[step % 2, chunk * TT:(chunk + 1) * TT].reshape(ROWS, D_MODEL)
        hm = _modulate(x, shift, scale).astype(_BF16)
        v = _layer_norm(_gelu_of_half(proj(hm, _O_V, _O_GA)), lnv_g_ref[...], lnv_b_ref[...])
        for g in range(SGU_GROUPS):
            v_s[chunk, g] = v[:, g * SGU_GROUP_DIM:(g + 1) * SGU_GROUP_DIM]
        xl_proj(hm, 0)
        xl_proj(hm, 1)
        return x, hm

    def recurrent_branch(hm):
        for p in range(n_pairs):
            xc = conv(p)
            if p + 2 < n_pairs:
                xl_proj(hm, p + 2)
            for chunk_cols in _RAW_CHUNKS[p]:
                proj_raw(hm, *chunk_cols)
            gates(p, xc)
            scan(p)

    def spatial_mix(chunk, g):
        if chunk == 0:
            w = w_sp_ref[g, 0:CHUNK, 0:CHUNK]
            vs = [v_s[0, g, seq_rows(b), :] for b in range(BG)]
        else:
            w = w_sp_ref[g, CHUNK:SGU_BLOCK, :]
            vs = [jnp.concatenate([v_s[0, g, seq_rows(b), :], v_s[1, g, seq_rows(b), :]], axis=0)
                  for b in range(BG)]
        out = _dot(w.astype(_BF16), jnp.concatenate(vs, axis=1).astype(_BF16))
        bias = b_sp_ref[g, chunk * CHUNK:(chunk + 1) * CHUNK, :]
        for b in range(BG):
            m_s[g, seq_rows(b), :] = out[:, b * SGU_GROUP_DIM:(b + 1) * SGU_GROUP_DIM] + bias

    def gating_branch(chunk, hm):
        gate_a_parts = []
        for g in range(SGU_GROUPS):
            spatial_mix(chunk, g)
            if g == 0:
                proj_raw(hm, _O_U + 512, _O_V)
            elif g <= 4:
                lo = _O_GA + (g - 1) * 256
                gate_a_parts.append(_sigmoid_of_half(proj(hm, lo, lo + 256)))
        gate_a = jnp.concatenate(gate_a_parts, axis=1)
        ya_in = (h_s[...] * _gelu_of_half(raw(_O_GLRU, _O_U))).astype(_BF16)
        return gate_a, ya_in

    def tail(chunk, x, hm, gate_a, y_a):
        gate_b = _sigmoid_of_half(proj(hm, _O_GB, D_IN))
        mixed = jnp.concatenate([m_s[g] for g in range(SGU_GROUPS)], axis=1)
        y_b = _dot((_gelu_of_half(raw(_O_U, _O_V)) * mixed).astype(_BF16), w_os_ref[...])
        merged = gate_a * y_a + gate_b * y_b
        mix = _dot(merged.astype(_BF16), w_out_ref[...])
        o_ref[chunk * ROWS:(chunk + 1) * ROWS, :] = ALPHA * x + _per_seq(mix, 1.0 + gate, jnp.multiply)

    x0, hm0 = head(0)
    recurrent_branch(hm0)
    gate_a0, ya_in0 = gating_branch(0, hm0)
    y_a0 = _dot(ya_in0, w_ol_ref[...])
    x1, hm1 = head(1)
    tail(0, x0, hm0, gate_a0, y_a0)
    recurrent_branch(hm1)
    gate_a1, ya_in1 = gating_branch(1, hm1)
    tail(1, x1, hm1, gate_a1, _dot(ya_in1, w_ol_ref[...]))


def _const_spec(shape):
    nd = len(shape)
    return pl.BlockSpec(shape, lambda *_: (0,) * nd, pipeline_mode=pl.Buffered(1))


def _mixer(x, mod, w_in, b_in, w_conv, b_conv, w_rg, b_rg, lam, w_sp, b_sp, lnv_g, lnv_b,
           w_ol, w_os, w_out):
    batch, seq, d = x.shape
    ngroups = batch // BG
    consts = (w_in, b_in, w_conv, b_conv, w_rg, b_rg, lam, w_sp, b_sp, lnv_g, lnv_b,
              w_ol, w_os, w_out)
    return pl.pallas_call(
        _mixer_kernel,
        grid=(ngroups, seq // SGU_BLOCK),
        in_specs=[
            pl.BlockSpec(memory_space=pl.ANY),
            pl.BlockSpec((None, BG, 3 * d), lambda g, i: (g, 0, 0)),
        ] + [_const_spec(a.shape) for a in consts],
        out_specs=pl.BlockSpec((None, 2 * ROWS, d), lambda g, i: (g, i, 0)),
        out_shape=jax.ShapeDtypeStruct((ngroups, seq * BG, d), _F32),
        scratch_shapes=[
            pltpu.VMEM((2, SGU_BLOCK, BG, d), _F32),
            pltpu.SemaphoreType.DMA((2,)),
            pltpu.VMEM((HALO + ROWS, LRU_WIDTH), _F32),
            pltpu.VMEM((ROWS, LRU_WIDTH), _F32),
            pltpu.VMEM((ROWS, LRU_WIDTH), _F32),
            pltpu.VMEM((BG, LRU_WIDTH), _F32),
            pltpu.VMEM((2, SGU_GROUPS, ROWS, SGU_GROUP_DIM), _F32),
            pltpu.VMEM((SGU_GROUPS, ROWS, SGU_GROUP_DIM), _F32),
            pltpu.VMEM((ROWS, _O_V - _O_GLRU), _F32),
        ],
        compiler_params=pltpu.CompilerParams(
            dimension_semantics=("arbitrary", "arbitrary"),
            vmem_limit_bytes=V7X_VMEM_LIMIT,
        ),
        name="token_mixer",
    )(x, mod, *consts)


def _mlp_kernel(r_ref, mod_ref, ln1_g_ref, ln1_b_ref, w_up_ref, w_down_ref, ln2_g_ref, ln2_b_ref,
                o_hbm, o_buf, o_sem):
    tiles_per_group = pl.num_programs(1)
    step = pl.program_id(0) * tiles_per_group + pl.program_id(1)
    last_step = pl.num_programs(0) * tiles_per_group - 1
    o_copies = functools.partial(_seq_tile_copies, o_hbm, o_buf, o_sem,
                                 tiles_per_group=tiles_per_group, tile_steps=MLP_TT, to_hbm=True)
    shift = mod_ref[:, 0:D_MODEL]
    scale = mod_ref[:, D_MODEL:2 * D_MODEL]
    gate = mod_ref[:, 2 * D_MODEL:3 * D_MODEL]

    @pl.when(step >= 2)
    def _():
        for cp in o_copies(step - 2):
            cp.wait()

    sub_start = [sum(MLP_SUB_ROWS[:r]) for r in range(len(MLP_SUB_ROWS) + 1)]

    def prep(r):
        rows = slice(sub_start[r], sub_start[r + 1])
        x = _layer_norm(r_ref[rows, :], ln1_g_ref[...], ln1_b_ref[...])
        return x, _modulate(x, shift, scale).astype(_BF16)

    def up(h2, k):
        cols = slice(k * FF_CHUNK, (k + 1) * FF_CHUNK)
        return jnp.square(jnp.maximum(_dot(h2, w_up_ref[:, cols]), 0.0)).astype(_BF16)

    def down(f, k):
        return _dot(f, w_down_ref[k * FF_CHUNK:(k + 1) * FF_CHUNK, :])

    n_sub = len(MLP_SUB_ROWS)
    n_chunks = D_FF // FF_CHUNK
    nxt = prep(0)
    f_first = up(nxt[1], 0)
    for r in range(n_sub):
        x, h2 = nxt
        acc = down(f_first, 0)
        for k in range(1, n_chunks):
            if k == n_chunks - 1 and r + 1 < n_sub:
                nxt = prep(r + 1)
                f_first = up(nxt[1], 0)
            acc = acc + down(up(h2, k), k)
        res = ALPHA * x + _per_seq(acc, 1.0 + gate, jnp.multiply)
        out = _layer_norm(res, ln2_g_ref[...], ln2_b_ref[...])
        steps = slice(sub_start[r] // BG, sub_start[r + 1] // BG)
        o_buf[step % 2, steps] = out.reshape(MLP_SUB_ROWS[r] // BG, BG, D_MODEL)

    for cp in o_copies(step):
        cp.start()

    @pl.when(step == last_step)
    def _():
        for cp in o_copies(step - 1) + o_copies(step):
            cp.wait()


def _mlp(rt, mod, ln1_g, ln1_b, w_up, w_down, ln2_g, ln2_b):
    ngroups, nrows, d = rt.shape
    seq = nrows // BG
    consts = (ln1_g, ln1_b, w_up, w_down, ln2_g, ln2_b)
    return pl.pallas_call(
        _mlp_kernel,
        grid=(ngroups, nrows // MLP_ROWS),
        in_specs=[
            pl.BlockSpec((None, MLP_ROWS, d), lambda g, i: (g, i, 0)),
            pl.BlockSpec((None, BG, 3 * d), lambda g, i: (g, 0, 1)),
        ] + [_const_spec(a.shape) for a in consts],
        out_specs=pl.BlockSpec(memory_space=pl.ANY),
        out_shape=jax.ShapeDtypeStruct((ngroups * BG, seq, d), _F32),
        scratch_shapes=[
            pltpu.VMEM((2, MLP_TT, BG, d), _F32),
            pltpu.SemaphoreType.DMA((2,)),
        ],
        compiler_params=pltpu.CompilerParams(
            dimension_semantics=("arbitrary", "arbitrary"),
            vmem_limit_bytes=V7X_VMEM_LIMIT,
        ),
        name="channel_mlp",
    )(rt, mod, *consts)


def kernel(x, c, w_ada, b_ada, w_in, b_in, w_conv, b_conv, w_rg_a, b_rg_a, w_rg_x, b_rg_x, lru_lambda,
           w_sp, b_sp, ln_v_g, ln_v_b, w_o_lru, w_o_sgu, w_out, ln1_g, ln1_b, w_up, w_down, ln2_g, ln2_b):
    batch, seq, d = x.shape
    depth = w_in.shape[0]
    ngroups = batch // BG
    row = lambda a: a.reshape(1, -1)

    assert batch % BG == 0 and seq % MLP_TT == 0 and (batch // BG) * (seq // MLP_TT) >= 2
    for l in range(depth):
        mod = _ada(c, w_ada[l], b_ada[l]).reshape(ngroups, BG, 6 * d)
        w_rg = (0.5 * jnp.concatenate([w_rg_a[l], w_rg_x[l]], axis=-1)).astype(_BF16)
        b_rg = 0.5 * jnp.stack([b_rg_a[l], b_rg_x[l]])
        half_cols = jnp.where(jnp.arange(D_IN) >= _O_GLRU, 0.5, 1.0).astype(_F32)
        b_sp_lanes = jnp.broadcast_to(b_sp[l][:, :, None], (SGU_GROUPS, SGU_BLOCK, SGU_GROUP_DIM))
        xt = _mixer(x, mod, (w_in[l] * half_cols).astype(_BF16), row(b_in[l] * half_cols),
                    w_conv[l], row(b_conv[l]), w_rg, b_rg,
                    row(lru_lambda[l]), w_sp[l], b_sp_lanes, row(ln_v_g[l]), row(ln_v_b[l]),
                    w_o_lru[l].astype(_BF16), w_o_sgu[l].astype(_BF16), w_out[l].astype(_BF16))
        x = _mlp(xt, mod, row(ln1_g[l]), row(ln1_b[l]), w_up[l].astype(_BF16), w_down[l].astype(_BF16),
                 row(ln2_g[l]), row(ln2_b[l]))
    return x
```

```python
import functools

import jax
import jax.numpy as jnp
import numpy as np
from jax import lax
from jax.experimental import pallas as pl
from jax.experimental.pallas import tpu as pltpu

D_MODEL = 1024
LRU_WIDTH = 1280
LRU_HEADS = 10
LRU_HEAD_DIM = LRU_WIDTH // LRU_HEADS
CONV_WIDTH = 4
LRU_C = 8.0
SGU_WIDTH = 768
SGU_GROUPS = 6
SGU_GROUP_DIM = SGU_WIDTH // SGU_GROUPS
SGU_BLOCK = 128
CHUNK = 64
D_FF = 4 * D_MODEL
D_IN = 2 * LRU_WIDTH + 2 * SGU_WIDTH + 2 * D_MODEL
ALPHA = 2.0 ** 0.25
LN_EPS = 1e-5

_O_XLRU, _O_GLRU, _O_U, _O_V, _O_GA, _O_GB = 0, 1280, 2560, 3328, 4096, 5120
_RAW_CHUNKS = (((1280, 1536),), ((1536, 1792),), ((1792, 2048),), ((2048, 2304), (2304, 2560)),
               ((2560, 2816), (2816, 3072)))

V7X_F32_SUBLANES = 8
BG = V7X_F32_SUBLANES
TT = CHUNK
ROWS = TT * BG
HALO = (CONV_WIDTH - 1) * BG
MLP_ROWS = 1024
MLP_TT = MLP_ROWS // BG
MLP_SUB_ROWS = (256,) * 4
FF_CHUNK = 1024
V7X_VMEM_LIMIT = 60 * 1024 * 1024

_F32 = jnp.float32
_BF16 = jnp.bfloat16
_GELU_C = np.sqrt(2.0 / np.pi).astype(np.float32)
_GELU_CK = _GELU_C * np.float32(0.044715)
_LOG2_E = np.float32(np.log2(np.e))
_TINY = 1e-30


def _sigmoid(x):
    return _sigmoid_of_half(0.5 * x)


def _sigmoid_of_half(half_x):
    return 0.5 * jnp.tanh(half_x) + 0.5


def _gelu_of_half(h):
    return h * jnp.tanh(h * (2.0 * _GELU_C + (8.0 * _GELU_CK) * (h * h))) + h


def _layer_norm(x, g, b):
    mu = jnp.mean(x, axis=-1, keepdims=True)
    xc = x - mu
    var = jnp.mean(xc * xc, axis=-1, keepdims=True)
    return xc * lax.rsqrt(var + LN_EPS) * g + b


def _per_seq(x, m, op):
    rows, d = x.shape
    return op(x.reshape(rows // BG, BG, d), m[None]).reshape(rows, d)


def _modulate(x, shift, scale):
    return _per_seq(_per_seq(x, 1.0 + scale, jnp.multiply), shift, jnp.add)


def _dot(a, b):
    return jnp.dot(a, b, preferred_element_type=_F32)


def _ada_kernel(c_ref, w_ref, b_ref, o_ref):
    c = c_ref[...]
    c_act = (c * _sigmoid(c)).astype(_BF16)
    o_ref[...] = _dot(c_act, w_ref[...].astype(_BF16)) + b_ref[...]


def _ada(c, w_ada, b_ada):
    b, d = c.shape
    n = w_ada.shape[1]
    bn = D_MODEL
    return pl.pallas_call(
        _ada_kernel,
        grid=(n // bn,),
        in_specs=[
            pl.BlockSpec((b, d), lambda j: (0, 0)),
            pl.BlockSpec((d, bn), lambda j: (0, j)),
            pl.BlockSpec((1, bn), lambda j: (0, j)),
        ],
        out_specs=pl.BlockSpec((b, bn), lambda j: (0, j)),
        out_shape=jax.ShapeDtypeStruct((b, n), _F32),
        name="ada_mod",
    )(c, w_ada, b_ada.reshape(1, n))


def _seq_tile_copies(hbm_ref, buf_ref, sem_ref, step, tiles_per_group, tile_steps, to_hbm):
    g = step // tiles_per_group
    t0 = (step % tiles_per_group) * tile_steps
    slot = step % 2
    copies = []
    for b in range(BG):
        hbm = hbm_ref.at[g * BG + b, pl.ds(t0, tile_steps), :]
        buf = buf_ref.at[slot, :, b, :]
        src, dst = (buf, hbm) if to_hbm else (hbm, buf)
        copies.append(pltpu.make_async_copy(src, dst, sem_ref.at[slot]))
    return copies


def _mixer_kernel(x_hbm, mod_ref, w_in_ref, b_in_ref, w_conv_ref, b_conv_ref, w_rg_ref, b_rg_ref,
                  lam_ref, w_sp_ref, b_sp_ref, lnv_g_ref, lnv_b_ref, w_ol_ref, w_os_ref, w_out_ref,
                  o_ref, x_buf, x_sem, xl_s, a_s, h_s, hc_s, v_s, m_s, raw_s):
    i = pl.program_id(1)
    steps_per_group = pl.num_programs(1)
    step = pl.program_id(0) * steps_per_group + i
    last_step = pl.num_programs(0) * steps_per_group - 1
    x_copies = functools.partial(_seq_tile_copies, x_hbm, x_buf, x_sem,
                                 tiles_per_group=steps_per_group, tile_steps=SGU_BLOCK, to_hbm=False)

    @pl.when(step == 0)
    def _():
        for cp in x_copies(step):
            cp.start()

    @pl.when(step < last_step)
    def _():
        for cp in x_copies(step + 1):
            cp.start()

    for cp in x_copies(step):
        cp.wait()

    @pl.when(i == 0)
    def _():
        xl_s[0:HALO, :] = jnp.zeros((HALO, LRU_WIDTH), _F32)
        hc_s[...] = jnp.zeros((BG, LRU_WIDTH), _F32)

    shift = mod_ref[:, 0:D_MODEL]
    scale = mod_ref[:, D_MODEL:2 * D_MODEL]
    gate = mod_ref[:, 2 * D_MODEL:3 * D_MODEL]
    z = -lam_ref[...]
    c_lam = -LRU_C * (jnp.maximum(z, 0.0) + jnp.log1p(jnp.exp(-jnp.abs(z))))
    half_c_log2e = (0.5 * _LOG2_E) * c_lam
    n_pairs = LRU_HEADS // 2

    def proj(hm, lo, hi):
        return _dot(hm, w_in_ref[:, lo:hi]) + b_in_ref[:, lo:hi]

    def proj_raw(hm, lo, hi):
        raw_s[:, lo - _O_GLRU:hi - _O_GLRU] = proj(hm, lo, hi)

    def raw(lo, hi):
        return raw_s[:, lo - _O_GLRU:hi - _O_GLRU]

    def pair_cols(p):
        return slice(2 * p * LRU_HEAD_DIM, (2 * p + 2) * LRU_HEAD_DIM)

    def xl_proj(hm, p):
        cols = pair_cols(p)
        xl_s[HALO:HALO + ROWS, cols] = proj(hm, cols.start, cols.stop)

    def conv(p):
        cols = pair_cols(p)
        xc = b_conv_ref[:, cols] + w_conv_ref[0:1, cols] * xl_s[0:ROWS, cols]
        for k in range(1, CONV_WIDTH):
            xc = xc + w_conv_ref[k:k + 1, cols] * xl_s[k * BG:k * BG + ROWS, cols]
        xl_s[0:HALO, cols] = xl_s[ROWS:ROWS + HALO, cols]
        return xc

    def gates(p, xc):
        cols = pair_cols(p)
        xcb = xc.astype(_BF16)
        for j in range(2):
            sl = slice(cols.start + j * LRU_HEAD_DIM, cols.start + (j + 1) * LRU_HEAD_DIM)
            loc = slice(j * LRU_HEAD_DIM, (j + 1) * LRU_HEAD_DIM)
            ri = _dot(xcb[:, loc], w_rg_ref[2 * p + j])
            tanh_r = jnp.tanh(ri[:, 0:LRU_HEAD_DIM] + b_rg_ref[0:1, sl])
            ig = _sigmoid_of_half(ri[:, LRU_HEAD_DIM:] + b_rg_ref[1:2, sl])
            a = jnp.exp2(half_c_log2e[:, sl] * tanh_r + half_c_log2e[:, sl])
            m2 = 1.0 - a * a
            mult = m2 * lax.rsqrt(jnp.maximum(m2, _TINY))
            a_s[:, sl] = a
            h_s[:, sl] = mult * (ig * xc[:, loc])

    def scan(p):
        cols = pair_cols(p)
        hprev = hc_s[:, cols]
        for t in range(TT):
            rows = slice(t * BG, (t + 1) * BG)
            hprev = a_s[rows, cols] * hprev + h_s[rows, cols]
            h_s[rows, cols] = hprev
        hc_s[:, cols] = hprev

    def seq_rows(b):
        return pl.ds(b, TT, stride=BG)

    def head(chunk):
        x = x_buf[step % 2, chunk * TT:(chunk + 1) * TT].reshape(ROWS, D_MODEL)
        hm = _modulate(x, shift, scale).astype(_BF16)
        xl_proj(hm, 0)
        xl_proj(hm, 1)
        v = _layer_norm(_gelu_of_half(proj(hm, _O_V, _O_GA)), lnv_g_ref[...], lnv_b_ref[...])
        for g in range(SGU_GROUPS):
            v_s[chunk, g] = v[:, g * SGU_GROUP_DIM:(g + 1) * SGU_GROUP_DIM]
        return x, hm

    def recurrent_branch(hm):
        for p in range(n_pairs):
            xc = conv(p)
            if p + 2 < n_pairs:
                xl_proj(hm, p + 2)
            for chunk_cols in _RAW_CHUNKS[p]:
                proj_raw(hm, *chunk_cols)
            gates(p, xc)
            scan(p)

    def spatial_mix(chunk, g):
        if chunk == 0:
            w = w_sp_ref[g, 0:CHUNK, 0:CHUNK]
            vs = [v_s[0, g, seq_rows(b), :] for b in range(BG)]
        else:
            w = w_sp_ref[g, CHUNK:SGU_BLOCK, :]
            vs = [jnp.concatenate([v_s[0, g, seq_rows(b), :], v_s[1, g, seq_rows(b), :]], axis=0)
                  for b in range(BG)]
        out = _dot(w.astype(_BF16), jnp.concatenate(vs, axis=1).astype(_BF16))
        bias = b_sp_ref[g, chunk * CHUNK:(chunk + 1) * CHUNK, :]
        for b in range(BG):
            m_s[g, seq_rows(b), :] = out[:, b * SGU_GROUP_DIM:(b + 1) * SGU_GROUP_DIM] + bias

    def gating_branch(chunk, hm):
        gate_a_parts = []
        for g in range(SGU_GROUPS):
            spatial_mix(chunk, g)
            if g == 0:
                proj_raw(hm, _O_U + 512, _O_V)
            elif g <= 4:
                lo = _O_GA + (g - 1) * 256
                gate_a_parts.append(_sigmoid_of_half(proj(hm, lo, lo + 256)))
        gate_a = jnp.concatenate(gate_a_parts, axis=1)
        ya_in = (h_s[...] * _gelu_of_half(raw(_O_GLRU, _O_U))).astype(_BF16)
        return gate_a, ya_in

    def tail(chunk, x, hm, gate_a, y_a):
        gate_b = _sigmoid_of_half(proj(hm, _O_GB, D_IN))
        mixed = jnp.concatenate([m_s[g] for g in range(SGU_GROUPS)], axis=1)
        y_b = _dot((_gelu_of_half(raw(_O_U, _O_V)) * mixed).astype(_BF16), w_os_ref[...])
        merged = gate_a * y_a + gate_b * y_b
        mix = _dot(merged.astype(_BF16), w_out_ref[...])
        o_ref[chunk * ROWS:(chunk + 1) * ROWS, :] = ALPHA * x + _per_seq(mix, 1.0 + gate, jnp.multiply)

    x0, hm0 = head(0)
    recurrent_branch(hm0)
    gate_a0, ya_in0 = gating_branch(0, hm0)
    y_a0 = _dot(ya_in0, w_ol_ref[...])
    x1, hm1 = head(1)
    tail(0, x0, hm0, gate_a0, y_a0)
    recurrent_branch(hm1)
    gate_a1, ya_in1 = gating_branch(1, hm1)
    tail(1, x1, hm1, gate_a1, _dot(ya_in1, w_ol_ref[...]))


def _const_spec(shape):
    nd = len(shape)
    return pl.BlockSpec(shape, lambda *_: (0,) * nd, pipeline_mode=pl.Buffered(1))


def _mixer(x, mod, w_in, b_in, w_conv, b_conv, w_rg, b_rg, lam, w_sp, b_sp, lnv_g, lnv_b,
           w_ol, w_os, w_out):
    batch, seq, d = x.shape
    ngroups = batch // BG
    consts = (w_in, b_in, w_conv, b_conv, w_rg, b_rg, lam, w_sp, b_sp, lnv_g, lnv_b,
              w_ol, w_os, w_out)
    return pl.pallas_call(
        _mixer_kernel,
        grid=(ngroups, seq // SGU_BLOCK),
        in_specs=[
            pl.BlockSpec(memory_space=pl.ANY),
            pl.BlockSpec((None, BG, 3 * d), lambda g, i: (g, 0, 0)),
        ] + [_const_spec(a.shape) for a in consts],
        out_specs=pl.BlockSpec((None, 2 * ROWS, d), lambda g, i: (g, i, 0)),
        out_shape=jax.ShapeDtypeStruct((ngroups, seq * BG, d), _F32),
        scratch_shapes=[
            pltpu.VMEM((2, SGU_BLOCK, BG, d), _F32),
            pltpu.SemaphoreType.DMA((2,)),
            pltpu.VMEM((HALO + ROWS, LRU_WIDTH), _F32),
            pltpu.VMEM((ROWS, LRU_WIDTH), _F32),
            pltpu.VMEM((ROWS, LRU_WIDTH), _F32),
            pltpu.VMEM((BG, LRU_WIDTH), _F32),
            pltpu.VMEM((2, SGU_GROUPS, ROWS, SGU_GROUP_DIM), _F32),
            pltpu.VMEM((SGU_GROUPS, ROWS, SGU_GROUP_DIM), _F32),
            pltpu.VMEM((ROWS, _O_V - _O_GLRU), _F32),
        ],
        compiler_params=pltpu.CompilerParams(
            dimension_semantics=("arbitrary", "arbitrary"),
            vmem_limit_bytes=V7X_VMEM_LIMIT,
        ),
        name="token_mixer",
    )(x, mod, *consts)


def _mlp_kernel(r_ref, mod_ref, ln1_g_ref, ln1_b_ref, w_up_ref, w_down_ref, ln2_g_ref, ln2_b_ref,
                o_hbm, o_buf, o_sem):
    tiles_per_group = pl.num_programs(1)
    step = pl.program_id(0) * tiles_per_group + pl.program_id(1)
    last_step = pl.num_programs(0) * tiles_per_group - 1
    o_copies = functools.partial(_seq_tile_copies, o_hbm, o_buf, o_sem,
                                 tiles_per_group=tiles_per_group, tile_steps=MLP_TT, to_hbm=True)
    shift = mod_ref[:, 0:D_MODEL]
    scale = mod_ref[:, D_MODEL:2 * D_MODEL]
    gate = mod_ref[:, 2 * D_MODEL:3 * D_MODEL]

    @pl.when(step >= 2)
    def _():
        for cp in o_copies(step - 2):
            cp.wait()

    sub_start = [sum(MLP_SUB_ROWS[:r]) for r in range(len(MLP_SUB_ROWS) + 1)]

    def prep(r):
        rows = slice(sub_start[r], sub_start[r + 1])
        x = _layer_norm(r_ref[rows, :], ln1_g_ref[...], ln1_b_ref[...])
        return x, _modulate(x, shift, scale).astype(_BF16)

    def up(h2, k):
        cols = slice(k * FF_CHUNK, (k + 1) * FF_CHUNK)
        return jnp.square(jnp.maximum(_dot(h2, w_up_ref[:, cols]), 0.0)).astype(_BF16)

    def down(f, k):
        return _dot(f, w_down_ref[k * FF_CHUNK:(k + 1) * FF_CHUNK, :])

    n_sub = len(MLP_SUB_ROWS)
    n_chunks = D_FF // FF_CHUNK
    nxt = prep(0)
    f_first = up(nxt[1], 0)
    for r in range(n_sub):
        x, h2 = nxt
        acc = down(f_first, 0)
        for k in range(1, n_chunks):
            if k == n_chunks - 1 and r + 1 < n_sub:
                nxt = prep(r + 1)
                f_first = up(nxt[1], 0)
            acc = acc + down(up(h2, k), k)
        res = ALPHA * x + _per_seq(acc, 1.0 + gate, jnp.multiply)
        out = _layer_norm(res, ln2_g_ref[...], ln2_b_ref[...])
        steps = slice(sub_start[r] // BG, sub_start[r + 1] // BG)
        o_buf[step % 2, steps] = out.reshape(MLP_SUB_ROWS[r] // BG, BG, D_MODEL)

    for cp in o_copies(step):
        cp.start()

    @pl.when(step == last_step)
    def _():
        for cp in o_copies(step - 1) + o_copies(step):
            cp.wait()


def _mlp(rt, mod, ln1_g, ln1_b, w_up, w_down, ln2_g, ln2_b):
    ngroups, nrows, d = rt.shape
    seq = nrows // BG
    consts = (ln1_g, ln1_b, w_up, w_down, ln2_g, ln2_b)
    return pl.pallas_call(
        _mlp_kernel,
        grid=(ngroups, nrows // MLP_ROWS),
        in_specs=[
            pl.BlockSpec((None, MLP_ROWS, d), lambda g, i: (g, i, 0)),
            pl.BlockSpec((None, BG, 3 * d), lambda g, i: (g, 0, 1)),
        ] + [_const_spec(a.shape) for a in consts],
        out_specs=pl.BlockSpec(memory_space=pl.ANY),
        out_shape=jax.ShapeDtypeStruct((ngroups * BG, seq, d), _F32),
        scratch_shapes=[
            pltpu.VMEM((2, MLP_TT, BG, d), _F32),
            pltpu.SemaphoreType.DMA((2,)),
        ],
        compiler_params=pltpu.CompilerParams(
            dimension_semantics=("arbitrary", "arbitrary"),
            vmem_limit_bytes=V7X_VMEM_LIMIT,
        ),
        name="channel_mlp",
    )(rt, mod, *consts)


def kernel(x, c, w_ada, b_ada, w_in, b_in, w_conv, b_conv, w_rg_a, b_rg_a, w_rg_x, b_rg_x, lru_lambda,
           w_sp, b_sp, ln_v_g, ln_v_b, w_o_lru, w_o_sgu, w_out, ln1_g, ln1_b, w_up, w_down, ln2_g, ln2_b):
    batch, seq, d = x.shape
    depth = w_in.shape[0]
    ngroups = batch // BG
    row = lambda a: a.reshape(1, -1)

    assert batch % BG == 0 and seq % MLP_TT == 0 and (batch // BG) * (seq // MLP_TT) >= 2
    for l in range(depth):
        mod = _ada(c, w_ada[l], b_ada[l]).reshape(ngroups, BG, 6 * d)
        w_rg = (0.5 * jnp.concatenate([w_rg_a[l], w_rg_x[l]], axis=-1)).astype(_BF16)
        b_rg = 0.5 * jnp.stack([b_rg_a[l], b_rg_x[l]])
        half_cols = jnp.where(jnp.arange(D_IN) >= _O_GLRU, 0.5, 1.0).astype(_F32)
        b_sp_lanes = jnp.broadcast_to(b_sp[l][:, :, None], (SGU_GROUPS, SGU_BLOCK, SGU_GROUP_DIM))
        xt = _mixer(x, mod, (w_in[l] * half_cols).astype(_BF16), row(b_in[l] * half_cols),
                    w_conv[l], row(b_conv[l]), w_rg, b_rg,
                    row(lru_lambda[l]), w_sp[l], b_sp_lanes, row(ln_v_g[l]), row(ln_v_b[l]),
                    w_o_lru[l].astype(_BF16), w_o_sgu[l].astype(_BF16), w_out[l].astype(_BF16))
        x = _mlp(xt, mod, row(ln1_g[l]), row(ln1_b[l]), w_up[l].astype(_BF16), w_down[l].astype(_BF16),
                 row(ln2_g[l]), row(ln2_b[l]))
    return x
```

```python
import functools

import jax
import jax.numpy as jnp
import numpy as np
from jax import lax
from jax.experimental import pallas as pl
from jax.experimental.pallas import tpu as pltpu

D_MODEL = 1024
LRU_WIDTH = 1280
LRU_HEADS = 10
LRU_HEAD_DIM = LRU_WIDTH // LRU_HEADS
CONV_WIDTH = 4
LRU_C = 8.0
SGU_WIDTH = 768
SGU_GROUPS = 6
SGU_GROUP_DIM = SGU_WIDTH // SGU_GROUPS
SGU_BLOCK = 128
CHUNK = 64
D_FF = 4 * D_MODEL
D_IN = 2 * LRU_WIDTH + 2 * SGU_WIDTH + 2 * D_MODEL
ALPHA = 2.0 ** 0.25
LN_EPS = 1e-5

_O_XLRU, _O_GLRU, _O_U, _O_V, _O_GA, _O_GB = 0, 1280, 2560, 3328, 4096, 5120
_RAW_CHUNKS = (((1280, 1536),), ((1536, 1792),), ((1792, 2048),), ((2048, 2304), (2304, 2560)),
               ((2560, 2816), (2816, 3072)))

V7X_F32_SUBLANES = 8
BG = V7X_F32_SUBLANES
TT = CHUNK
ROWS = TT * BG
HALO = (CONV_WIDTH - 1) * BG
MLP_ROWS = 1024
MLP_TT = MLP_ROWS // BG
MLP_SUB_ROWS = (256,) * 4
FF_CHUNK = 1024
V7X_VMEM_LIMIT = 60 * 1024 * 1024

_F32 = jnp.float32
_BF16 = jnp.bfloat16
_GELU_C = np.sqrt(2.0 / np.pi).astype(np.float32)
_GELU_CK = _GELU_C * np.float32(0.044715)
_LOG2_E = np.float32(np.log2(np.e))
_TINY = 1e-30


def _sigmoid(x):
    return _sigmoid_of_half(0.5 * x)


def _sigmoid_of_half(half_x):
    return 0.5 * jnp.tanh(half_x) + 0.5


def _gelu_of_half(h):
    return h * jnp.tanh(h * (2.0 * _GELU_C + (8.0 * _GELU_CK) * (h * h))) + h


def _layer_norm(x, g, b):
    mu = jnp.mean(x, axis=-1, keepdims=True)
    xc = x - mu
    var = jnp.mean(xc * xc, axis=-1, keepdims=True)
    return xc * lax.rsqrt(var + LN_EPS) * g + b


def _per_seq(x, m, op):
    rows, d = x.shape
    return op(x.reshape(rows // BG, BG, d), m[None]).reshape(rows, d)


def _modulate(x, shift, scale):
    return _per_seq(_per_seq(x, 1.0 + scale, jnp.multiply), shift, jnp.add)


def _dot(a, b):
    return jnp.dot(a, b, preferred_element_type=_F32)


def _ada_kernel(c_ref, w_ref, b_ref, o_ref):
    c = c_ref[...]
    c_act = (c * _sigmoid(c)).astype(_BF16)
    o_ref[...] = _dot(c_act, w_ref[...].astype(_BF16)) + b_ref[...]


def _ada(c, w_ada, b_ada):
    b, d = c.shape
    n = w_ada.shape[1]
    bn = D_MODEL
    return pl.pallas_call(
        _ada_kernel,
        grid=(n // bn,),
        in_specs=[
            pl.BlockSpec((b, d), lambda j: (0, 0)),
            pl.BlockSpec((d, bn), lambda j: (0, j)),
            pl.BlockSpec((1, bn), lambda j: (0, j)),
        ],
        out_specs=pl.BlockSpec((b, bn), lambda j: (0, j)),
        out_shape=jax.ShapeDtypeStruct((b, n), _F32),
        name="ada_mod",
    )(c, w_ada, b_ada.reshape(1, n))


def _seq_tile_copies(hbm_ref, buf_ref, sem_ref, step, tiles_per_group, tile_steps, to_hbm):
    g = step // tiles_per_group
    t0 = (step % tiles_per_group) * tile_steps
    slot = step % 2
    copies = []
    for b in range(BG):
        hbm = hbm_ref.at[g * BG + b, pl.ds(t0, tile_steps), :]
        buf = buf_ref.at[slot, :, b, :]
        src, dst = (buf, hbm) if to_hbm else (hbm, buf)
        copies.append(pltpu.make_async_copy(src, dst, sem_ref.at[slot]))
    return copies


def _mixer_kernel(x_hbm, mod_ref, w_in_ref, b_in_ref, w_conv_ref, b_conv_ref, w_rg_ref, b_rg_ref,
                  lam_ref, w_sp_ref, b_sp_ref, lnv_g_ref, lnv_b_ref, w_ol_ref, w_os_ref, w_out_ref,
                  o_ref, x_buf, x_sem, xl_s, a_s, h_s, hc_s, v_s, m_s, raw_s):
    i = pl.program_id(1)
    steps_per_group = pl.num_programs(1)
    step = pl.program_id(0) * steps_per_group + i
    last_step = pl.num_programs(0) * steps_per_group - 1
    x_copies = functools.partial(_seq_tile_copies, x_hbm, x_buf, x_sem,
                                 tiles_per_group=steps_per_group, tile_steps=SGU_BLOCK, to_hbm=False)

    @pl.when(step == 0)
    def _():
        for cp in x_copies(step):
            cp.start()

    @pl.when(step < last_step)
    def _():
        for cp in x_copies(step + 1):
            cp.start()

    for cp in x_copies(step):
        cp.wait()

    @pl.when(i == 0)
    def _():
        xl_s[0:HALO, :] = jnp.zeros((HALO, LRU_WIDTH), _F32)
        hc_s[...] = jnp.zeros((BG, LRU_WIDTH), _F32)

    shift = mod_ref[:, 0:D_MODEL]
    scale = mod_ref[:, D_MODEL:2 * D_MODEL]
    gate = mod_ref[:, 2 * D_MODEL:3 * D_MODEL]
    z = -lam_ref[...]
    c_lam = -LRU_C * (jnp.maximum(z, 0.0) + jnp.log1p(jnp.exp(-jnp.abs(z))))
    half_c_log2e = (0.5 * _LOG2_E) * c_lam
    n_pairs = LRU_HEADS // 2

    def proj(hm, lo, hi):
        return _dot(hm, w_in_ref[:, lo:hi]) + b_in_ref[:, lo:hi]

    def proj_raw(hm, lo, hi):
        raw_s[:, lo - _O_GLRU:hi - _O_GLRU] = proj(hm, lo, hi)

    def raw(lo, hi):
        return raw_s[:, lo - _O_GLRU:hi - _O_GLRU]

    def pair_cols(p):
        return slice(2 * p * LRU_HEAD_DIM, (2 * p + 2) * LRU_HEAD_DIM)

    def xl_proj(hm, p):
        cols = pair_cols(p)
        xl_s[HALO:HALO + ROWS, cols] = proj(hm, cols.start, cols.stop)

    def conv(p):
        cols = pair_cols(p)
        xc = b_conv_ref[:, cols] + w_conv_ref[0:1, cols] * xl_s[0:ROWS, cols]
        for k in range(1, CONV_WIDTH):
            xc = xc + w_conv_ref[k:k + 1, cols] * xl_s[k * BG:k * BG + ROWS, cols]
        xl_s[0:HALO, cols] = xl_s[ROWS:ROWS + HALO, cols]
        return xc

    def gates(p, xc):
        cols = pair_cols(p)
        xcb = xc.astype(_BF16)
        for j in range(2):
            sl = slice(cols.start + j * LRU_HEAD_DIM, cols.start + (j + 1) * LRU_HEAD_DIM)
            loc = slice(j * LRU_HEAD_DIM, (j + 1) * LRU_HEAD_DIM)
            ri = _dot(xcb[:, loc], w_rg_ref[2 * p + j])
            tanh_r = jnp.tanh(ri[:, 0:LRU_HEAD_DIM] + b_rg_ref[0:1, sl])
            ig = _sigmoid_of_half(ri[:, LRU_HEAD_DIM:] + b_rg_ref[1:2, sl])
            a = jnp.exp2(half_c_log2e[:, sl] * tanh_r + half_c_log2e[:, sl])
            m2 = 1.0 - a * a
            mult = m2 * lax.rsqrt(jnp.maximum(m2, _TINY))
            a_s[:, sl] = a
            h_s[:, sl] = mult * (ig * xc[:, loc])

    def scan(p):
        cols = pair_cols(p)
        hprev = hc_s[:, cols]
        for t in range(TT):
            rows = slice(t * BG, (t + 1) * BG)
            hprev = a_s[rows, cols] * hprev + h_s[rows, cols]
            h_s[rows, cols] = hprev
        hc_s[:, cols] = hprev

    def seq_rows(b):
        return pl.ds(b, TT, stride=BG)

    def head(chunk):
        x = x_buf[step % 2, chunk * TT:(chunk + 1) * TT].reshape(ROWS, D_MODEL)
        hm = _modulate(x, shift, scale).astype(_BF16)
        xl_proj(hm, 0)
        xl_proj(hm, 1)
        v = _layer_norm(_gelu_of_half(proj(hm, _O_V, _O_GA)), lnv_g_ref[...], lnv_b_ref[...])
        for g in range(SGU_GROUPS):
            v_s[chunk, g] = v[:, g * SGU_GROUP_DIM:(g + 1) * SGU_GROUP_DIM]
        return x, hm

    def recurrent_branch(hm):
        for p in range(n_pairs):
            xc = conv(p)
            if p + 2 < n_pairs:
                xl_proj(hm, p + 2)
            for chunk_cols in _RAW_CHUNKS[p]:
                proj_raw(hm, *chunk_cols)
            gates(p, xc)
            scan(p)

    def spatial_mix(chunk, g):
        if chunk == 0:
            w = w_sp_ref[g, 0:CHUNK, 0:CHUNK]
            vs = [v_s[0, g, seq_rows(b), :] for b in range(BG)]
        else:
            w = w_sp_ref[g, CHUNK:SGU_BLOCK, :]
            vs = [jnp.concatenate([v_s[0, g, seq_rows(b), :], v_s[1, g, seq_rows(b), :]], axis=0)
                  for b in range(BG)]
        out = _dot(w.astype(_BF16), jnp.concatenate(vs, axis=1).astype(_BF16))
        bias = b_sp_ref[g, chunk * CHUNK:(chunk + 1) * CHUNK, :]
        for b in range(BG):
            m_s[g, seq_rows(b), :] = out[:, b * SGU_GROUP_DIM:(b + 1) * SGU_GROUP_DIM] + bias

    def gating_branch(chunk, hm):
        gate_a_parts = []
        for g in range(SGU_GROUPS):
            spatial_mix(chunk, g)
            if g == 0:
                proj_raw(hm, _O_U + 512, _O_V)
            elif g <= 4:
                lo = _O_GA + (g - 1) * 256
                gate_a_parts.append(_sigmoid_of_half(proj(hm, lo, lo + 256)))
        gate_a = jnp.concatenate(gate_a_parts, axis=1)
        ya_in = (h_s[...] * _gelu_of_half(raw(_O_GLRU, _O_U))).astype(_BF16)
        return gate_a, ya_in

    def tail(chunk, x, hm, gate_a, y_a):
        gate_b = _sigmoid_of_half(proj(hm, _O_GB, D_IN))
        mixed = jnp.concatenate([m_s[g] for g in range(SGU_GROUPS)], axis=1)
        y_b = _dot((_gelu_of_half(raw(_O_U, _O_V)) * mixed).astype(_BF16), w_os_ref[...])
        merged = gate_a * y_a + gate_b * y_b
        mix = _dot(merged.astype(_BF16), w_out_ref[...])
        o_ref[chunk * ROWS:(chunk + 1) * ROWS, :] = ALPHA * x + _per_seq(mix, 1.0 + gate, jnp.multiply)

    x0, hm0 = head(0)
    recurrent_branch(hm0)
    gate_a0, ya_in0 = gating_branch(0, hm0)
    y_a0 = _dot(ya_in0, w_ol_ref[...])
    x1, hm1 = head(1)
    tail(0, x0, hm0, gate_a0, y_a0)
    recurrent_branch(hm1)
    gate_a1, ya_in1 = gating_branch(1, hm1)
    tail(1, x1, hm1, gate_a1, _dot(ya_in1, w_ol_ref[...]))


def _const_spec(shape):
    nd = len(shape)
    return pl.BlockSpec(shape, lambda *_: (0,) * nd, pipeline_mode=pl.Buffered(1))


def _mixer(x, mod, w_in, b_in, w_conv, b_conv, w_rg, b_rg, lam, w_sp, b_sp, lnv_g, lnv_b,
           w_ol, w_os, w_out):
    batch, seq, d = x.shape
    ngroups = batch // BG
    consts = (w_in, b_in, w_conv, b_conv, w_rg, b_rg, lam, w_sp, b_sp, lnv_g, lnv_b,
              w_ol, w_os, w_out)
    return pl.pallas_call(
        _mixer_kernel,
        grid=(ngroups, seq // SGU_BLOCK),
        in_specs=[
            pl.BlockSpec(memory_space=pl.ANY),
            pl.BlockSpec((None, BG, 3 * d), lambda g, i: (g, 0, 0)),
        ] + [_const_spec(a.shape) for a in consts],
        out_specs=pl.BlockSpec((None, 2 * ROWS, d), lambda g, i: (g, i, 0)),
        out_shape=jax.ShapeDtypeStruct((ngroups, seq * BG, d), _F32),
        scratch_shapes=[
            pltpu.VMEM((2, SGU_BLOCK, BG, d), _F32),
            pltpu.SemaphoreType.DMA((2,)),
            pltpu.VMEM((HALO + ROWS, LRU_WIDTH), _F32),
            pltpu.VMEM((ROWS, LRU_WIDTH), _F32),
            pltpu.VMEM((ROWS, LRU_WIDTH), _F32),
            pltpu.VMEM((BG, LRU_WIDTH), _F32),
            pltpu.VMEM((2, SGU_GROUPS, ROWS, SGU_GROUP_DIM), _F32),
            pltpu.VMEM((SGU_GROUPS, ROWS, SGU_GROUP_DIM), _F32),
            pltpu.VMEM((ROWS, _O_V - _O_GLRU), _F32),
        ],
        compiler_params=pltpu.CompilerParams(
            dimension_semantics=("arbitrary", "arbitrary"),
            vmem_limit_bytes=V7X_VMEM_LIMIT,
            allow_input_fusion=[False, False] + [a.dtype == _BF16 for a in consts],
        ),
        name="token_mixer",
    )(x, mod, *consts)


def _mlp_kernel(r_ref, mod_ref, ln1_g_ref, ln1_b_ref, w_up_ref, w_down_ref, ln2_g_ref, ln2_b_ref,
                o_hbm, o_buf, o_sem):
    tiles_per_group = pl.num_programs(1)
    step = pl.program_id(0) * tiles_per_group + pl.program_id(1)
    last_step = pl.num_programs(0) * tiles_per_group - 1
    o_copies = functools.partial(_seq_tile_copies, o_hbm, o_buf, o_sem,
                                 tiles_per_group=tiles_per_group, tile_steps=MLP_TT, to_hbm=True)
    shift = mod_ref[:, 0:D_MODEL]
    scale = mod_ref[:, D_MODEL:2 * D_MODEL]
    gate = mod_ref[:, 2 * D_MODEL:3 * D_MODEL]

    @pl.when(step >= 2)
    def _():
        for cp in o_copies(step - 2):
            cp.wait()

    sub_start = [sum(MLP_SUB_ROWS[:r]) for r in range(len(MLP_SUB_ROWS) + 1)]

    def prep(r):
        rows = slice(sub_start[r], sub_start[r + 1])
        x = _layer_norm(r_ref[rows, :], ln1_g_ref[...], ln1_b_ref[...])
        return x, _modulate(x, shift, scale).astype(_BF16)

    def up(h2, k):
        cols = slice(k * FF_CHUNK, (k + 1) * FF_CHUNK)
        return jnp.square(jnp.maximum(_dot(h2, w_up_ref[:, cols]), 0.0)).astype(_BF16)

    def down(f, k):
        return _dot(f, w_down_ref[k * FF_CHUNK:(k + 1) * FF_CHUNK, :])

    n_sub = len(MLP_SUB_ROWS)
    n_chunks = D_FF // FF_CHUNK
    nxt = prep(0)
    f_first = up(nxt[1], 0)
    for r in range(n_sub):
        x, h2 = nxt
        acc = down(f_first, 0)
        for k in range(1, n_chunks):
            if k == n_chunks - 1 and r + 1 < n_sub:
                nxt = prep(r + 1)
                f_first = up(nxt[1], 0)
            acc = acc + down(up(h2, k), k)
        res = ALPHA * x + _per_seq(acc, 1.0 + gate, jnp.multiply)
        out = _layer_norm(res, ln2_g_ref[...], ln2_b_ref[...])
        steps = slice(sub_start[r] // BG, sub_start[r + 1] // BG)
        o_buf[step % 2, steps] = out.reshape(MLP_SUB_ROWS[r] // BG, BG, D_MODEL)

    for cp in o_copies(step):
        cp.start()

    @pl.when(step == last_step)
    def _():
        for cp in o_copies(step - 1) + o_copies(step):
            cp.wait()


def _mlp(rt, mod, ln1_g, ln1_b, w_up, w_down, ln2_g, ln2_b):
    ngroups, nrows, d = rt.shape
    seq = nrows // BG
    consts = (ln1_g, ln1_b, w_up, w_down, ln2_g, ln2_b)
    return pl.pallas_call(
        _mlp_kernel,
        grid=(ngroups, nrows // MLP_ROWS),
        in_specs=[
            pl.BlockSpec((None, MLP_ROWS, d), lambda g, i: (g, i, 0)),
            pl.BlockSpec((None, BG, 3 * d), lambda g, i: (g, 0, 1)),
        ] + [_const_spec(a.shape) for a in consts],
        out_specs=pl.BlockSpec(memory_space=pl.ANY),
        out_shape=jax.ShapeDtypeStruct((ngroups * BG, seq, d), _F32),
        scratch_shapes=[
            pltpu.VMEM((2, MLP_TT, BG, d), _F32),
            pltpu.SemaphoreType.DMA((2,)),
        ],
        compiler_params=pltpu.CompilerParams(
            dimension_semantics=("arbitrary", "arbitrary"),
            vmem_limit_bytes=V7X_VMEM_LIMIT,
            allow_input_fusion=[False, False] + [a.dtype == _BF16 for a in consts],
        ),
        name="channel_mlp",
    )(rt, mod, *consts)


def kernel(x, c, w_ada, b_ada, w_in, b_in, w_conv, b_conv, w_rg_a, b_rg_a, w_rg_x, b_rg_x, lru_lambda,
           w_sp, b_sp, ln_v_g, ln_v_b, w_o_lru, w_o_sgu, w_out, ln1_g, ln1_b, w_up, w_down, ln2_g, ln2_b):
    batch, seq, d = x.shape
    depth = w_in.shape[0]
    ngroups = batch // BG
    row = lambda a: a.reshape(1, -1)

    assert batch % BG == 0 and seq % MLP_TT == 0 and (batch // BG) * (seq // MLP_TT) >= 2
    for l in range(depth):
        mod = _ada(c, w_ada[l], b_ada[l]).reshape(ngroups, BG, 6 * d)
        w_rg = (0.5 * jnp.concatenate([w_rg_a[l], w_rg_x[l]], axis=-1)).astype(_BF16)
        b_rg = 0.5 * jnp.stack([b_rg_a[l], b_rg_x[l]])
        half_cols = jnp.where(jnp.arange(D_IN) >= _O_GLRU, 0.5, 1.0).astype(_F32)
        b_sp_lanes = jnp.broadcast_to(b_sp[l][:, :, None], (SGU_GROUPS, SGU_BLOCK, SGU_GROUP_DIM))
        xt = _mixer(x, mod, (w_in[l] * half_cols).astype(_BF16), row(b_in[l] * half_cols),
                    w_conv[l], row(b_conv[l]), w_rg, b_rg,
                    row(lru_lambda[l]), w_sp[l], b_sp_lanes, row(ln_v_g[l]), row(ln_v_b[l]),
                    w_o_lru[l].astype(_BF16), w_o_sgu[l].astype(_BF16), w_out[l].astype(_BF16))
        x = _mlp(xt, mod, row(ln1_g[l]), row(ln1_b[l]), w_up[l].astype(_BF16), w_down[l].astype(_BF16),
                 row(ln2_g[l]), row(ln2_b[l]))
    return x
```
